```python
import jax, jax.numpy as jnp
from jax import lax
import numpy as np

D_MODEL = 2048
BATCH = 4
SEQ = 2048
DEPTH = 4
DEC_BATCH = 128
DEC_SEQ = 4
PAST_LEN = 16384
PAGE_SIZE = 128

SSM_EXPAND = 2
SSM_INNER = SSM_EXPAND * D_MODEL
SSM_HEADDIM = 64
SSM_HEADS = SSM_INNER // SSM_HEADDIM
SSM_GROUPS = 8
SSM_STATE = 128
SSM_CONV = 4
SSM_CHUNK = 128
SSM_CONV_DIM = SSM_INNER + 2 * SSM_GROUPS * SSM_STATE
SGU_WIDTH = D_MODEL
SGU_GROUPS = 8
SGU_CHUNK = 128
SC_WIDTH = D_MODEL
SC_CONV = 3
N_BRANCH = 3
FFN_HIDDEN = -(-8 * D_MODEL // (3 * 256)) * 256
N_IN = SSM_INNER + SSM_CONV_DIM + SSM_HEADS + 2 * SGU_WIDTH + 3 * SC_WIDTH + N_BRANCH * D_MODEL
RMS_EPS = 1e-6
LN_EPS = 1e-5

kernel_name = "hybrid_ssd_sgu_shortconv_decode_step"


def rmsnorm(x, w):
    xf = x.astype(jnp.float32)
    y = xf * lax.rsqrt(jnp.mean(xf * xf, axis=-1, keepdims=True) + RMS_EPS)
    return (y * w.astype(jnp.float32)).astype(x.dtype)


def layernorm(x, w, b):
    xf = x.astype(jnp.float32)
    mu = jnp.mean(xf, axis=-1, keepdims=True)
    var = jnp.mean(jnp.square(xf - mu), axis=-1, keepdims=True)
    y = (xf - mu) * lax.rsqrt(var + LN_EPS)
    return (y * w.astype(jnp.float32) + b.astype(jnp.float32)).astype(x.dtype)


def gated_group_rmsnorm(y, z, w):
    g = (y.astype(jnp.float32) * jax.nn.silu(z.astype(jnp.float32)))
    shp = g.shape
    g = g.reshape(shp[:-1] + (SSM_GROUPS, SSM_INNER // SSM_GROUPS))
    g = g * lax.rsqrt(jnp.mean(g * g, axis=-1, keepdims=True) + RMS_EPS)
    return (g.reshape(shp) * w.astype(jnp.float32)).astype(z.dtype)


def split_proj(proj):
    sizes = (SSM_INNER, SSM_CONV_DIM, SSM_HEADS, SGU_WIDTH, SGU_WIDTH, SC_WIDTH, SC_WIDTH, SC_WIDTH, N_BRANCH * D_MODEL)
    idx = np.cumsum(np.array(sizes))[:-1].tolist()
    return jnp.split(proj, idx, axis=-1)


def causal_dwconv(x, prefix, w, b):
    k = w.shape[0]
    t = x.shape[1]
    xp = jnp.concatenate([prefix.astype(x.dtype), x], axis=1)
    y = xp[:, 0:t] * w[0]
    for j in range(1, k):
        y = y + xp[:, j:j + t] * w[j]
    if b is not None:
        y = y + b
    return y, xp[:, -(k - 1):]


def ssd_scan(x, dt, a, b_mat, c_mat, h0, chunk):
    bsz, t, nh, p = x.shape
    g, n = b_mat.shape[2], b_mat.shape[3]
    hg = nh // g
    nc = t // chunk
    f32 = jnp.float32
    xc = x.astype(f32).reshape(bsz, nc, chunk, g, hg, p)
    dtc = dt.astype(f32).reshape(bsz, nc, chunk, g, hg)
    bc = b_mat.astype(f32).reshape(bsz, nc, chunk, g, n)
    cc = c_mat.astype(f32).reshape(bsz, nc, chunk, g, n)
    acum = jnp.cumsum(dtc * a.astype(f32).reshape(g, hg), axis=2)
    seg = acum[:, :, :, None] - acum[:, :, None, :]
    mask = jnp.tril(jnp.ones((chunk, chunk), bool))[:, :, None, None]
    decay = jnp.where(mask, jnp.exp(jnp.where(mask, seg, 0.0)), 0.0)
    xdt = xc * dtc[..., None]
    cb = jnp.einsum("bclgn,bcsgn->bclsg", cc, bc)
    y_diag = jnp.einsum("bclsg,bclsgh,bcsghp->bclghp", cb, decay, xdt)
    decay_last = jnp.exp(acum[:, :, -1:] - acum)
    chunk_states = jnp.einsum("bclgn,bclgh,bclghp->bcghpn", bc, decay_last, xdt)
    chunk_decay = jnp.exp(acum[:, :, -1])

    def step(h, inp):
        s, d = inp
        return h * d[..., None, None] + s, h

    h_init = h0.astype(f32).reshape(bsz, g, hg, p, n)
    h_final, h_enter = lax.scan(step, h_init, (jnp.moveaxis(chunk_states, 1, 0), jnp.moveaxis(chunk_decay, 1, 0)))
    h_enter = jnp.moveaxis(h_enter, 0, 1)
    y_off = jnp.einsum("bclgn,bcghpn,bclgh->bclghp", cc, h_enter, jnp.exp(acum))
    y = (y_diag + y_off).reshape(bsz, t, nh, p)
    return y, h_final.reshape(bsz, nh, p, n).astype(h0.dtype)


def sgu_mix(v, w_s, b_s, offset):
    bsz, t, _ = v.shape
    l = min(t, SGU_CHUNK)
    nc = t // l
    pos = (offset + jnp.arange(l)) % SGU_CHUNK
    w_causal = jnp.tril(w_s)
    w_sub = w_causal[:, pos[:, None], pos[None, :]]
    bias = jnp.transpose(b_s[:, pos])
    vg = v.reshape(bsz, nc, l, SGU_GROUPS, SGU_WIDTH // SGU_GROUPS)
    mixed = jnp.einsum("gts,bcsgd->bctgd", w_sub, vg) + bias[None, None, :, :, None]
    return mixed.reshape(bsz, t, SGU_WIDTH)


def decoder_layer(x, conv_a_prefix, ssm_h0, conv_c_prefix, sgu_offset,
                  norm_mix_w, w_in, conv_a_w, conv_a_b, dt_bias, a_log, d_skip, ssm_norm_w, w_out_a,
                  sgu_ln_w, sgu_ln_b, w_spatial, b_spatial, w_out_b, conv_c_w, w_out_c, w_o,
                  norm_ffn_w, w_gate, w_up, w_down):
    bsz, t, _ = x.shape
    h = rmsnorm(x, norm_mix_w)
    z, xbc, dt_raw, u, v, sc_b, sc_c, sc_h, gate_logits = split_proj(h @ w_in)

    xbc, conv_a_new = causal_dwconv(xbc, conv_a_prefix, conv_a_w, conv_a_b)
    xbc = jax.nn.silu(xbc)
    xs, bm, cm = jnp.split(xbc, [SSM_INNER, SSM_INNER + SSM_GROUPS * SSM_STATE], axis=-1)
    dt = jax.nn.softplus(dt_raw.astype(jnp.float32) + dt_bias.astype(jnp.float32))
    a = -jnp.exp(a_log.astype(jnp.float32))
    xh = xs.reshape(bsz, t, SSM_HEADS, SSM_HEADDIM)
    y_ssm, ssm_new = ssd_scan(xh, dt, a,
                              bm.reshape(bsz, t, SSM_GROUPS, SSM_STATE),
                              cm.reshape(bsz, t, SSM_GROUPS, SSM_STATE),
                              ssm_h0, min(t, SSM_CHUNK))
    y_ssm = y_ssm + d_skip.astype(jnp.float32)[:, None] * xh.astype(jnp.float32)
    y_ssm = gated_group_rmsnorm(y_ssm.reshape(bsz, t, SSM_INNER), z, ssm_norm_w)
    branch_a = y_ssm @ w_out_a

    u = jax.nn.gelu(u)
    v = layernorm(jax.nn.gelu(v), sgu_ln_w, sgu_ln_b)
    branch_b = (u * sgu_mix(v, w_spatial, b_spatial, sgu_offset)) @ w_out_b

    conv_c, conv_c_new = causal_dwconv(sc_c * sc_h, conv_c_prefix, conv_c_w, None)
    branch_c = (sc_b * conv_c) @ w_out_c

    g_a, g_b, g_c = jnp.split(jax.nn.sigmoid(gate_logits), N_BRANCH, axis=-1)
    x = x + (g_a * branch_a + g_b * branch_b + g_c * branch_c) @ w_o

    hf = rmsnorm(x, norm_ffn_w)
    x = x + (jax.nn.silu(hf @ w_gate) * (hf @ w_up)) @ w_down
    return x, conv_a_new, ssm_new, conv_c_new, v


def setup_inputs(seed: int = 0) -> dict:
    key = jax.random.key(seed)
    ks = jax.random.split(key, 32)
    f32 = jnp.float32

    def nrm(k, shape, scale):
        return jax.random.normal(k, shape, f32) * scale

    dt0 = jnp.exp(jax.random.uniform(ks[10], (DEPTH, SSM_HEADS), f32) * (np.log(0.1) - np.log(0.001)) + np.log(0.001))
    dt_bias = dt0 + jnp.log(-jnp.expm1(-dt0))
    a_log = jnp.log(jax.random.uniform(ks[11], (DEPTH, SSM_HEADS), f32, minval=1.0, maxval=16.0))
    return {
        "x_prompt": nrm(ks[0], (BATCH, SEQ, D_MODEL), 1.0),
        "x_sample": nrm(ks[1], (DEC_BATCH, DEC_SEQ, D_MODEL), 1.0),
        "state_conv_a": nrm(ks[2], (DEPTH, DEC_BATCH, SSM_CONV - 1, SSM_CONV_DIM), 1.0),
        "state_ssm": nrm(ks[3], (DEPTH, DEC_BATCH, SSM_HEADS, SSM_HEADDIM, SSM_STATE), 0.1),
        "state_conv_c": nrm(ks[4], (DEPTH, DEC_BATCH, SC_CONV - 1, SC_WIDTH), 1.0),
        "norm_mix_w": 1.0 + nrm(ks[5], (DEPTH, D_MODEL), 0.02),
        "w_in": nrm(ks[6], (DEPTH, D_MODEL, N_IN), D_MODEL ** -0.5),
        "conv_a_w": nrm(ks[7], (DEPTH, SSM_CONV, SSM_CONV_DIM), SSM_CONV ** -0.5),
        "conv_a_b": nrm(ks[8], (DEPTH, SSM_CONV_DIM), 0.02),
        "dt_bias": dt_bias,
        "a_log": a_log,
        "d_skip": 1.0 + nrm(ks[9], (DEPTH, SSM_HEADS), 0.02),
        "ssm_norm_w": 1.0 + nrm(ks[12], (DEPTH, SSM_INNER), 0.02),
        "w_out_a": nrm(ks[13], (DEPTH, SSM_INNER, D_MODEL), SSM_INNER ** -0.5),
        "sgu_ln_w": 1.0 + nrm(ks[14], (DEPTH, SGU_WIDTH), 0.02),
        "sgu_ln_b": nrm(ks[15], (DEPTH, SGU_WIDTH), 0.02),
        "w_spatial": nrm(ks[16], (DEPTH, SGU_GROUPS, SGU_CHUNK, SGU_CHUNK), SGU_CHUNK ** -0.5),
        "b_spatial": 1.0 + nrm(ks[17], (DEPTH, SGU_GROUPS, SGU_CHUNK), 0.02),
        "w_out_b": nrm(ks[18], (DEPTH, SGU_WIDTH, D_MODEL), SGU_WIDTH ** -0.5),
        "conv_c_w": nrm(ks[19], (DEPTH, SC_CONV, SC_WIDTH), SC_CONV ** -0.5),
        "w_out_c": nrm(ks[20], (DEPTH, SC_WIDTH, D_MODEL), SC_WIDTH ** -0.5),
        "w_o": nrm(ks[21], (DEPTH, D_MODEL, D_MODEL), D_MODEL ** -0.5),
        "norm_ffn_w": 1.0 + nrm(ks[22], (DEPTH, D_MODEL), 0.02),
        "w_gate": nrm(ks[23], (DEPTH, D_MODEL, FFN_HIDDEN), D_MODEL ** -0.5),
        "w_up": nrm(ks[24], (DEPTH, D_MODEL, FFN_HIDDEN), D_MODEL ** -0.5),
        "w_down": nrm(ks[25], (DEPTH, FFN_HIDDEN, D_MODEL), FFN_HIDDEN ** -0.5),
        "norm_final_w": 1.0 + nrm(ks[26], (D_MODEL,), 0.02),
    }


def reference(x_prompt, x_sample, state_conv_a, state_ssm, state_conv_c,
              norm_mix_w, w_in, conv_a_w, conv_a_b, dt_bias, a_log, d_skip, ssm_norm_w, w_out_a,
              sgu_ln_w, sgu_ln_b, w_spatial, b_spatial, w_out_b, conv_c_w, w_out_c, w_o,
              norm_ffn_w, w_gate, w_up, w_down, norm_final_w):
    bp = x_prompt.shape[0]
    dtype = x_prompt.dtype
    zero_conv_a = jnp.zeros((bp, SSM_CONV - 1, SSM_CONV_DIM), dtype)
    zero_ssm = jnp.zeros((bp, SSM_HEADS, SSM_HEADDIM, SSM_STATE), dtype)
    zero_conv_c = jnp.zeros((bp, SC_CONV - 1, SC_WIDTH), dtype)

    yp, ys = x_prompt, x_sample
    ca_p, ssm_p, cc_p = [], [], []
    ca_s, ssm_s, cc_s, v_s = [], [], [], []
    for i in range(DEPTH):
        lp = (norm_mix_w[i], w_in[i], conv_a_w[i], conv_a_b[i], dt_bias[i], a_log[i], d_skip[i],
              ssm_norm_w[i], w_out_a[i], sgu_ln_w[i], sgu_ln_b[i], w_spatial[i], b_spatial[i],
              w_out_b[i], conv_c_w[i], w_out_c[i], w_o[i], norm_ffn_w[i], w_gate[i], w_up[i], w_down[i])
        yp, ca, sp, cc, _ = decoder_layer(yp, zero_conv_a, zero_ssm, zero_conv_c, 0, *lp)
        ca_p.append(ca)
        ssm_p.append(sp)
        cc_p.append(cc)
        ys, ca, sp, cc, v = decoder_layer(ys, state_conv_a[i], state_ssm[i], state_conv_c[i], PAST_LEN, *lp)
        ca_s.append(ca)
        ssm_s.append(sp)
        cc_s.append(cc)
        v_s.append(v)

    y_prompt = rmsnorm(yp, norm_final_w)
    y_sample = rmsnorm(ys, norm_final_w)
    return (y_prompt, y_sample,
            jnp.stack(ca_p), jnp.stack(ssm_p), jnp.stack(cc_p),
            jnp.stack(ca_s), jnp.stack(ssm_s), jnp.stack(cc_s), jnp.stack(v_s))
```

```python
import functools

import jax
import jax.numpy as jnp
from jax import lax
from jax.experimental import pallas as pl
from jax.experimental.pallas import tpu as pltpu

F32 = jnp.float32
BF16 = jnp.bfloat16

RMS_EPS = 1e-6
LN_EPS = 1e-5
PAST_LEN = 16384
CHUNK = 128
LANES = 128
SUBLANES = 8
BF16_ROWS = 16
NEG_BIG = -1e30
VMEM_LIMIT_BYTES = 56 * 1024 * 1024


def _cparams(*sem):
    return pltpu.CompilerParams(dimension_semantics=sem, vmem_limit_bytes=VMEM_LIMIT_BYTES)


def _dot(a, b):
    return jnp.dot(a, b, preferred_element_type=F32)


def _dot_nt(a, b):
    return lax.dot_general(a, b, (((1,), (1,)), ((), ())), preferred_element_type=F32)


def _dot_tn(a, b):
    return lax.dot_general(a, b, (((0,), (0,)), ((), ())), preferred_element_type=F32)


def _split3(x):
    hi = x.astype(BF16)
    r = x - hi.astype(F32)
    mid = r.astype(BF16)
    lo = (r - mid.astype(F32)).astype(BF16)
    return hi, mid, lo


def _dot_exact_rhs(x, m_bf16):
    hi, mid, lo = _split3(x)
    return _dot(hi, m_bf16) + _dot(mid, m_bf16) + _dot(lo, m_bf16)


def _dot_exact_lhs(m_bf16, x):
    hi, mid, lo = _split3(x)
    return _dot(m_bf16, hi) + _dot(m_bf16, mid) + _dot(m_bf16, lo)


def _silu(x):
    return x * jax.nn.sigmoid(x)


def _softplus(x):
    return jnp.maximum(x, 0.0) + jnp.log1p(jnp.exp(-jnp.abs(x)))


def _rms(x, w):
    return x * lax.rsqrt(jnp.mean(x * x, axis=-1, keepdims=True) + RMS_EPS) * w


def _layernorm(x, w, b):
    mu = jnp.mean(x, axis=-1, keepdims=True)
    d = x - mu
    var = jnp.mean(d * d, axis=-1, keepdims=True)
    return d * lax.rsqrt(var + LN_EPS) * w + b


def _inproj_body(x_ref, nw_ref, w_ref, wdh_ref, wdl_ref, o_ref, dt_ref, h_scr, *, sub):
    tm = x_ref.shape[0]

    @pl.when(pl.program_id(1) == 0)
    def _():
        def rows(i, carry):
            r = pl.ds(pl.multiple_of(i * sub, sub), sub)
            h = _rms(x_ref[r, :], nw_ref[...])
            hb = h.astype(BF16)
            h_scr[r, :] = hb
            hl = (h - hb.astype(F32)).astype(BF16)
            dt_ref[r, :] = _dot(hb, wdh_ref[...]) + _dot(hb, wdl_ref[...]) + _dot(hl, wdh_ref[...])
            return carry
        lax.fori_loop(0, tm // sub, rows, 0)

    o_ref[...] = _dot(h_scr[...], w_ref[...])


def _inproj(x, nw, w, wdh, wdl, *, tm, tn):
    rows, d = x.shape
    n = w.shape[1]
    return pl.pallas_call(
        functools.partial(_inproj_body, sub=min(tm, 128)),
        grid=(rows // tm, n // tn),
        in_specs=[
            pl.BlockSpec((tm, d), lambda m, j: (m, 0)),
            pl.BlockSpec((1, d), lambda m, j: (0, 0)),
            pl.BlockSpec((d, tn), lambda m, j: (0, j)),
            pl.BlockSpec((d, LANES), lambda m, j: (0, 0)),
            pl.BlockSpec((d, LANES), lambda m, j: (0, 0)),
        ],
        out_specs=[
            pl.BlockSpec((tm, tn), lambda m, j: (m, j)),
            pl.BlockSpec((tm, LANES), lambda m, j: (m, 0)),
        ],
        out_shape=[jax.ShapeDtypeStruct((rows, n), F32), jax.ShapeDtypeStruct((rows, LANES), F32)],
        scratch_shapes=[pltpu.VMEM((tm, d), BF16)],
        compiler_params=_cparams("parallel", "arbitrary"),
        name="inproj",
    )(x, nw, w, wdh, wdl)


def _ssd_prompt_body(z_ref, x_ref, bc_ref, dt_ref, cw_ref, cb_ref, dtb_ref, alog_ref, dsk_ref, nw_ref,
                     e_ref, ya_ref, cst_ref, h_ref, xpad, bcpad, xs_scr, bcs_scr, *, G, HG, P, N):
    c = pl.program_id(1)
    L = CHUNK
    GW = HG * P
    I = G * GW
    K = cw_ref.shape[0]
    T0 = SUBLANES - (K - 1)

    @pl.when(c == 0)
    def _():
        h_ref[...] = jnp.zeros(h_ref.shape, F32)
        xpad[0:SUBLANES, :] = jnp.zeros((SUBLANES, I), F32)
        bcpad[0:SUBLANES, :] = jnp.zeros((SUBLANES, 2 * G * N), F32)

    xpad[SUBLANES:SUBLANES + L, :] = x_ref[...]
    bcpad[SUBLANES:SUBLANES + L, :] = bc_ref[...]

    def conv(pad, lo, hi, col0):
        acc = pad[T0:T0 + L, lo:hi] * cw_ref[0:1, col0 + lo:col0 + hi]
        for j in range(1, K):
            acc = acc + pad[T0 + j:T0 + j + L, lo:hi] * cw_ref[j:j + 1, col0 + lo:col0 + hi]
        acc = acc + cb_ref[:, col0 + lo:col0 + hi]
        return _silu(acc)

    for g in range(G):
        xs_scr[:, g * GW:(g + 1) * GW] = conv(xpad, g * GW, (g + 1) * GW, 0)
    for g in range(2 * G):
        bcs_scr[:, g * N:(g + 1) * N] = conv(bcpad, g * N, (g + 1) * N, I)

    xpad[0:SUBLANES, :] = xpad[L:L + SUBLANES, :]
    bcpad[0:SUBLANES, :] = bcpad[L:L + SUBLANES, :]

    @pl.when(c == pl.num_programs(1) - 1)
    def _():
        cst_ref[0, :, 0:I] = xpad[L:L + SUBLANES, :]
        cst_ref[0, :, I:] = bcpad[L:L + SUBLANES, :]

    dt = _softplus(dt_ref[...] + dtb_ref[...])
    a = -jnp.exp(alog_ref[...])
    row = lax.broadcasted_iota(jnp.int32, (L, L), 0)
    col = lax.broadcasted_iota(jnp.int32, (L, L), 1)
    tri = row >= col
    tril_ones = jnp.where(tri, 1.0, 0.0).astype(BF16)
    acum = _dot_exact_lhs(tril_ones, dt * a)
    acum_t = acum.T
    last = acum[L - 1:L, :]
    dl = jnp.exp(last - acum)
    ea = jnp.exp(acum)
    cd_t = jnp.exp(acum_t[:, L - 1:L])
    dt_s, dl_s, ea_s = _split3(dt), _split3(dl), _split3(ea)
    lane = lax.broadcasted_iota(jnp.int32, (L, LANES), 1)
    HPL = LANES // P

    for g in range(G):
        gs = slice(g * GW, (g + 1) * GW)
        e_g = e_ref[:, gs]

        def expand(parts):
            return _dot(parts[0], e_g) + _dot(parts[1], e_g) + _dot(parts[2], e_g)

        xg = xs_scr[:, gs]
        xdt = xg * expand(dt_s)
        bb = bcs_scr[:, g * N:(g + 1) * N].astype(BF16)
        cb16 = bcs_scr[:, (G + g) * N:(G + g + 1) * N].astype(BF16)
        cbm = _dot_nt(cb16, bb)
        hg16 = h_ref[0, gs, :].astype(BF16)
        y = _dot_nt(cb16, hg16) * expand(ea_s) + dsk_ref[:, gs] * xg

        tiles = []
        for j in range(GW // LANES):
            xt = xdt[:, j * LANES:(j + 1) * LANES]
            acc = jnp.zeros((L, LANES), F32)
            for hh in range(HPL):
                h = g * HG + j * HPL + hh
                seg = acum[:, h:h + 1] - acum_t[h:h + 1, :]
                dec = jnp.exp(jnp.where(tri, seg, NEG_BIG))
                m = (cbm * dec).astype(BF16)
                inhead = (lane >= hh * P) & (lane < (hh + 1) * P)
                acc = acc + _dot(m, jnp.where(inhead, xt, 0.0).astype(BF16))
            tiles.append(acc)
        y = y + jnp.concatenate(tiles, axis=1)

        gt = y * _silu(z_ref[:, gs])
        ya_ref[:, gs] = _rms(gt, nw_ref[:, gs]).astype(BF16)

        s_new = _dot_tn((xdt * expand(dl_s)).astype(BF16), bb)
        for hh in range(HG):
            h = g * HG + hh
            rs = slice(g * GW + hh * P, g * GW + (hh + 1) * P)
            h_ref[0, rs, :] = h_ref[0, rs, :] * cd_t[h:h + 1, :] + s_new[hh * P:(hh + 1) * P, :]


def _ssd_prompt(proj, dt_raw, lw, cfg, batch, seq):
    G, HG, P, N = cfg["G"], cfg["HG"], cfg["P"], cfg["N"]
    I = G * HG * P
    BC = 2 * G * N
    CD = I + BC
    nc = seq // CHUNK
    off = cfg["off"]
    row = lambda b, c: b * nc + c
    full = lambda b, c: (0, 0)
    return pl.pallas_call(
        functools.partial(_ssd_prompt_body, G=G, HG=HG, P=P, N=N),
        grid=(batch, nc),
        in_specs=[
            pl.BlockSpec((CHUNK, I), lambda b, c: (row(b, c), off["z"] // I)),
            pl.BlockSpec((CHUNK, I), lambda b, c: (row(b, c), off["x"] // I)),
            pl.BlockSpec((CHUNK, BC), lambda b, c: (row(b, c), off["bc"] // BC)),
            pl.BlockSpec((CHUNK, LANES), lambda b, c: (row(b, c), 0)),
            pl.BlockSpec(lw["conv_a_w"].shape, full),
            pl.BlockSpec((1, CD), full),
            pl.BlockSpec((1, LANES), full),
            pl.BlockSpec((1, LANES), full),
            pl.BlockSpec((1, I), full),
            pl.BlockSpec((1, I), full),
            pl.BlockSpec((LANES, I), full),
        ],
        out_specs=[
            pl.BlockSpec((CHUNK, I), lambda b, c: (row(b, c), 0)),
            pl.BlockSpec((1, SUBLANES, CD), lambda b, c: (b, 0, 0)),
            pl.BlockSpec((1, I, N), lambda b, c: (b, 0, 0)),
        ],
        out_shape=[
            jax.ShapeDtypeStruct((batch * seq, I), BF16),
            jax.ShapeDtypeStruct((batch, SUBLANES, CD), F32),
            jax.ShapeDtypeStruct((batch, I, N), F32),
        ],
        scratch_shapes=[
            pltpu.VMEM((CHUNK + SUBLANES, I), F32),
            pltpu.VMEM((CHUNK + SUBLANES, BC), F32),
            pltpu.VMEM((CHUNK, I), F32),
            pltpu.VMEM((CHUNK, BC), F32),
        ],
        compiler_params=_cparams("parallel", "arbitrary"),
        name="ssd_prompt",
    )(proj, proj, proj, dt_raw, lw["conv_a_w"], lw["conv_a_b"], lw["dt_bias"], lw["a_log"],
      lw["d_skip_e"], lw["ssm_norm_w"], lw["expand"])


def _ssd_sample_a_body(x_ref, b_ref, c_ref, dt_ref, px_ref, pb_ref, pc_ref, cwx_ref, cwb_ref, cwc_ref,
                       cbx_ref, cbb_ref, cbc_ref, dtb_ref, alog_ref, dsk_ref, e_ref,
                       yp_ref, xdl_ref, ea_ref, bs_ref, cs_ref, cd_ref, nx_ref, nb_ref, nc_ref):
    T = x_ref.shape[0]
    K = cwx_ref.shape[0]

    def conv(p_ref, cur_ref, w_ref, bias_ref, new_ref):
        rows = [p_ref[j] for j in range(K - 1)] + [cur_ref[l] for l in range(T)]
        for j in range(K - 1):
            new_ref[j] = rows[T + j]
        outs = []
        for l in range(T):
            acc = rows[l] * w_ref[0:1, :]
            for j in range(1, K):
                acc = acc + rows[l + j] * w_ref[j:j + 1, :]
            outs.append(_silu(acc + bias_ref[...]))
        return outs

    xs = conv(px_ref, x_ref, cwx_ref, cbx_ref, nx_ref)
    bs = conv(pb_ref, b_ref, cwb_ref, cbb_ref, nb_ref)
    cs = conv(pc_ref, c_ref, cwc_ref, cbc_ref, nc_ref)

    a = -jnp.exp(alog_ref[...])
    dts, acums = [], []
    run = None
    for l in range(T):
        dt = _softplus(dt_ref[l] + dtb_ref[...])
        run = dt * a if run is None else run + dt * a
        dts.append(dt)
        acums.append(run)
    last = acums[T - 1]
    cd_ref[...] = jnp.exp(last)

    e_g = e_ref[...]

    def expand(coef):
        return _dot_exact_rhs(coef, e_g)

    xdts = [xs[l] * expand(dts[l]) for l in range(T)]
    for l in range(T):
        bs_ref[l] = bs[l].astype(BF16)
        cs_ref[l] = cs[l].astype(BF16)
        ea_ref[l] = expand(jnp.exp(acums[l]))
        xdl_ref[l] = xdts[l] * expand(jnp.exp(last - acums[l]))
        y = dsk_ref[...] * xs[l]
        for s in range(l + 1):
            cb = jnp.sum(cs[l] * bs[s], axis=-1, keepdims=True)
            y = y + expand(cb * jnp.exp(acums[l] - acums[s])) * xdts[s]
        yp_ref[l] = y


def _ssd_sample_a(proj3, dt3, conv_tm, lw, cfg):
    G, HG, P, N = cfg["G"], cfg["HG"], cfg["P"], cfg["N"]
    GW = HG * P
    I = G * GW
    T, SQ, _ = proj3.shape
    K = lw["conv_a_w"].shape[0]
    off = cfg["off"]
    xo, bo, co = off["x"] // GW, off["bc"] // N, off["bc"] // N + G
    sxo, sbo, sco = 0, I // N, I // N + G
    tok = lambda w, o: pl.BlockSpec((T, SQ, w), lambda g: (0, 0, o + g))
    pre = lambda w, o: pl.BlockSpec((K - 1, SQ, w), lambda g: (0, 0, o + g))
    par = lambda r, w, o: pl.BlockSpec((r, w), lambda g: (0, o + g))
    fixed = lambda shape: pl.BlockSpec(shape, lambda g: tuple(0 for _ in shape))
    return pl.pallas_call(
        _ssd_sample_a_body,
        grid=(G,),
        in_specs=[
            tok(GW, xo), tok(N, bo), tok(N, co), fixed((T, SQ, LANES)),
            pre(GW, sxo), pre(N, sbo), pre(N, sco),
            par(K, GW, sxo), par(K, N, sbo), par(K, N, sco),
            par(1, GW, sxo), par(1, N, sbo), par(1, N, sco),
            fixed((1, LANES)), fixed((1, LANES)), par(1, GW, 0), par(LANES, GW, 0),
        ],
        out_specs=[
            tok(GW, 0), tok(GW, 0), tok(GW, 0), tok(N, 0), tok(N, 0), fixed((SQ, LANES)),
            pre(GW, 0), pre(N, 0), pre(N, 0),
        ],
        out_shape=[
            jax.ShapeDtypeStruct((T, SQ, I), F32),
            jax.ShapeDtypeStruct((T, SQ, I), F32),
            jax.ShapeDtypeStruct((T, SQ, I), F32),
            jax.ShapeDtypeStruct((T, SQ, G * N), BF16),
            jax.ShapeDtypeStruct((T, SQ, G * N), BF16),
            jax.ShapeDtypeStruct((SQ, LANES), F32),
            jax.ShapeDtypeStruct((K - 1, SQ, I), F32),
            jax.ShapeDtypeStruct((K - 1, SQ, G * N), F32),
            jax.ShapeDtypeStruct((K - 1, SQ, G * N), F32),
        ],
        compiler_params=_cparams("arbitrary"),
        name="ssd_sample_a",
    )(proj3, proj3, proj3, dt3, conv_tm, conv_tm, conv_tm,
      lw["conv_a_w"], lw["conv_a_w"], lw["conv_a_w"], lw["conv_a_b"], lw["conv_a_b"], lw["conv_a_b"],
      lw["dt_bias"], lw["a_log"], lw["d_skip_e"], lw["expand"])


def _ssd_sample_b_body(cd_ref, h0_ref, cs_ref, bs_ref, xdl_ref, ea_ref, yp_ref, z_ref, nw_ref,
                       ya_ref, hn_ref, *, HG, P, H):
    sb = pl.program_id(0)
    g = pl.program_id(1)
    SB = h0_ref.shape[0]
    TP = yp_ref.shape[1]
    for i in range(SB):
        hmat = h0_ref[i]
        yoff = _dot_nt(cs_ref[i], hmat.astype(BF16))
        y = yp_ref[i] + yoff[0:TP, :] * ea_ref[i]
        gt = y * _silu(z_ref[i])
        ya_ref[i] = _rms(gt, nw_ref[...])
        s_new = _dot_tn(xdl_ref[i], bs_ref[i])
        for hh in range(HG):
            cd = cd_ref[(sb * SB + i) * H + g * HG + hh]
            rs = slice(hh * P, (hh + 1) * P)
            hn_ref[i, rs, :] = hmat[rs, :] * cd + s_new[rs, :]


def _ssd_sample_b(cd_flat, h0, cs_t, bs_t, xdl_t, ea_t, yp_t, z_t, nw, cfg, sb):
    G, HG, P, N = cfg["G"], cfg["HG"], cfg["P"], cfg["N"]
    GW = HG * P
    SQ, I, _ = h0.shape
    TP = yp_t.shape[1]
    tok = lambda r, w: pl.BlockSpec((sb, r, w), lambda s, g: (s, 0, g))
    return pl.pallas_call(
        functools.partial(_ssd_sample_b_body, HG=HG, P=P, H=G * HG),
        grid=(SQ // sb, G),
        in_specs=[
            pl.BlockSpec(memory_space=pltpu.SMEM),
            pl.BlockSpec((sb, GW, N), lambda s, g: (s, g, 0)),
            tok(BF16_ROWS, N), tok(BF16_ROWS, N), tok(BF16_ROWS, GW),
            tok(TP, GW), tok(TP, GW), tok(TP, GW),
            pl.BlockSpec((1, GW), lambda s, g: (0, g)),
        ],
        out_specs=[
            tok(TP, GW),
            pl.BlockSpec((sb, GW, N), lambda s, g: (s, g, 0)),
        ],
        out_shape=[
            jax.ShapeDtypeStruct((SQ, TP, I), F32),
            jax.ShapeDtypeStruct((SQ, I, N), F32),
        ],
        compiler_params=_cparams("parallel", "parallel"),
        name="ssd_sample_b",
    )(cd_flat, h0, cs_t, bs_t, xdl_t, ea_t, yp_t, z_t, nw)


def _mix_prompt_body(u_ref, v_ref, sb_ref, sc_ref, sh_ref, lnw_ref, lnb_ref, ws_ref, be_ref, cw_ref,
                     yb_ref, yc_ref, cst_ref, qpad, vb_scr):
    c = pl.program_id(1)
    R, W = u_ref.shape
    SG = ws_ref.shape[0]
    GWS = W // SG
    K = cw_ref.shape[0]
    T0 = SUBLANES - (K - 1)

    @pl.when(c == 0)
    def _():
        qpad[0:SUBLANES, :] = jnp.zeros((SUBLANES, W), F32)

    row = lax.broadcasted_iota(jnp.int32, (CHUNK, CHUNK), 0)
    col = lax.broadcasted_iota(jnp.int32, (CHUNK, CHUNK), 1)
    tri = row >= col
    for j in range(R // CHUNK):
        rs = slice(j * CHUNK, (j + 1) * CHUNK)
        v = _layernorm(jax.nn.gelu(v_ref[rs, :]), lnw_ref[...], lnb_ref[...])
        vb_scr[...] = v.astype(BF16)
        for g in range(SG):
            gs = slice(g * GWS, (g + 1) * GWS)
            w = jnp.where(tri, ws_ref[g], 0.0).astype(BF16)
            mixed = _dot(w, vb_scr[:, gs]) + be_ref[:, gs]
            yb_ref[rs, gs] = (jax.nn.gelu(u_ref[rs, gs]) * mixed).astype(BF16)

    qpad[SUBLANES:SUBLANES + R, :] = sc_ref[...] * sh_ref[...]
    conv = qpad[T0:T0 + R, :] * cw_ref[0:1, :]
    for j in range(1, K):
        conv = conv + qpad[T0 + j:T0 + j + R, :] * cw_ref[j:j + 1, :]
    yc_ref[...] = (sb_ref[...] * conv).astype(BF16)
    qpad[0:SUBLANES, :] = qpad[R:R + SUBLANES, :]

    @pl.when(c == pl.num_programs(1) - 1)
    def _():
        cst_ref[0] = qpad[R:R + SUBLANES, :]


def _mix_prompt(proj, lw, cfg, batch, seq, r):
    W = cfg["W"]
    off = cfg["off"]
    nr = seq // r
    tok = lambda name: pl.BlockSpec((r, W), lambda b, c: (b * nr + c, off[name] // W))
    fixed = lambda shape: pl.BlockSpec(shape, lambda b, c: tuple(0 for _ in shape))
    out = pl.BlockSpec((r, W), lambda b, c: (b * nr + c, 0))
    return pl.pallas_call(
        _mix_prompt_body,
        grid=(batch, nr),
        in_specs=[
            tok("u"), tok("v"), tok("scb"), tok("scc"), tok("sch"),
            fixed((1, W)), fixed((1, W)), fixed(lw["w_spatial"].shape), fixed((CHUNK, W)),
            fixed(lw["conv_c_w"].shape),
        ],
        out_specs=[out, out, pl.BlockSpec((1, SUBLANES, W), lambda b, c: (b, 0, 0))],
        out_shape=[
            jax.ShapeDtypeStruct((batch * seq, W), BF16),
            jax.ShapeDtypeStruct((batch * seq, W), BF16),
            jax.ShapeDtypeStruct((batch, SUBLANES, W), F32),
        ],
        scratch_shapes=[pltpu.VMEM((r + SUBLANES, W), F32), pltpu.VMEM((CHUNK, W), BF16)],
        compiler_params=_cparams("parallel", "arbitrary"),
        name="mix_prompt",
    )(proj, proj, proj, proj, proj, lw["sgu_ln_w"], lw["sgu_ln_b"], lw["w_spatial"], lw["b_spatial_e"],
      lw["conv_c_w"])


def _mix_sample_stats_body(v_ref, lnw_ref, lnb_ref, vo_ref):
    for l in range(v_ref.shape[0]):
        vo_ref[l] = _layernorm(jax.nn.gelu(v_ref[l]), lnw_ref[...], lnb_ref[...])


def _mix_sample_body(u_ref, v_ref, sb_ref, sc_ref, sh_ref, pq_ref, we_ref, be_ref, cw_ref,
                     yb_ref, yc_ref, cst_ref):
    T = u_ref.shape[0]
    K = cw_ref.shape[0]
    vs = [v_ref[l] for l in range(T)]
    for l in range(T):
        mixed = be_ref[l:l + 1, :]
        for s in range(T):
            mixed = mixed + we_ref[l * T + s:l * T + s + 1, :] * vs[s]
        yb_ref[l] = (jax.nn.gelu(u_ref[l]) * mixed).astype(BF16)
    rows = [pq_ref[j] for j in range(K - 1)] + [sc_ref[l] * sh_ref[l] for l in range(T)]
    for j in range(K - 1):
        cst_ref[j] = rows[T + j]
    for l in range(T):
        conv = rows[l] * cw_ref[0:1, :]
        for j in range(1, K):
            conv = conv + rows[l + j] * cw_ref[j:j + 1, :]
        yc_ref[l] = (sb_ref[l] * conv).astype(BF16)


def _mix_sample(proj3, convc_tm, lw, cfg, wc):
    W = cfg["W"]
    off = cfg["off"]
    T, SQ, _ = proj3.shape
    K = lw["conv_c_w"].shape[0]
    v_norm = pl.pallas_call(
        _mix_sample_stats_body,
        grid=(1,),
        in_specs=[
            pl.BlockSpec((T, SQ, W), lambda i: (0, 0, off["v"] // W)),
            pl.BlockSpec((1, W), lambda i: (0, 0)),
            pl.BlockSpec((1, W), lambda i: (0, 0)),
        ],
        out_specs=pl.BlockSpec((T, SQ, W), lambda i: (0, 0, 0)),
        out_shape=jax.ShapeDtypeStruct((T, SQ, W), F32),
        compiler_params=_cparams("arbitrary"),
        name="mix_sample_norm",
    )(proj3, lw["sgu_ln_w"], lw["sgu_ln_b"])
    tok = lambda name: pl.BlockSpec((T, SQ, wc), lambda j: (0, 0, off[name] // wc + j))
    loc = lambda r: pl.BlockSpec((r, SQ, wc), lambda j: (0, 0, j))
    par = lambda r: pl.BlockSpec((r, wc), lambda j: (0, j))
    yb, yc, cst = pl.pallas_call(
        _mix_sample_body,
        grid=(W // wc,),
        in_specs=[tok("u"), loc(T), tok("scb"), tok("scc"), tok("sch"), loc(K - 1),
                  par(T * T), par(T), par(K)],
        out_specs=[loc(T), loc(T), loc(K - 1)],
        out_shape=[
            jax.ShapeDtypeStruct((T, SQ, W), BF16),
            jax.ShapeDtypeStruct((T, SQ, W), BF16),
            jax.ShapeDtypeStruct((K - 1, SQ, W), F32),
        ],
        compiler_params=_cparams("parallel"),
        name="mix_sample",
    )(proj3, v_norm, proj3, proj3, proj3, convc_tm, lw["w_sub_e"], lw["b_sub_e"], lw["conv_c_w"])
    return yb, yc, cst, v_norm


def _merge_body(ya_ref, yb_ref, yc_ref, ga_ref, gb_ref, gc_ref, wa_ref, wb_ref, wc_ref, wo_ref, x_ref, o_ref):
    @pl.when(pl.program_id(1) == 0)
    def _():
        o_ref[...] = x_ref[...]

    merged = (jax.nn.sigmoid(ga_ref[...]) * _dot(ya_ref[...], wa_ref[...])
              + jax.nn.sigmoid(gb_ref[...]) * _dot(yb_ref[...], wb_ref[...])
              + jax.nn.sigmoid(gc_ref[...]) * _dot(yc_ref[...], wc_ref[...]))
    o_ref[...] += _dot(merged.astype(BF16), wo_ref[...])


def _merge(ya, yb, yc, proj, x, lw, cfg, *, tm, tn):
    rows, d = x.shape
    I, W, SCW = ya.shape[1], yb.shape[1], yc.shape[1]
    go = cfg["off"]["gate"] // tn
    act = lambda w: pl.BlockSpec((tm, w), lambda m, j: (m, 0))
    gate = lambda k: pl.BlockSpec((tm, tn), lambda m, j: (m, go + k * (d // tn) + j))
    wcol = lambda k: pl.BlockSpec((k, tn), lambda m, j: (0, j))
    return pl.pallas_call(
        _merge_body,
        grid=(rows // tm, d // tn),
        in_specs=[act(I), act(W), act(SCW), gate(0), gate(1), gate(2),
                  wcol(I), wcol(W), wcol(SCW), pl.BlockSpec((tn, d), lambda m, j: (j, 0)), act(d)],
        out_specs=act(d),
        out_shape=jax.ShapeDtypeStruct((rows, d), F32),
        compiler_params=_cparams("parallel", "arbitrary"),
        name="merge",
    )(ya, yb, yc, proj, proj, proj, lw["w_out_a"], lw["w_out_b"], lw["w_out_c"], lw["w_o"], x)


def _ffn_body(x_ref, nw_ref, wg_ref, wu_ref, wd_ref, fw_ref, o_ref, h_scr, *, sub, final):
    f = pl.program_id(1)
    tm = x_ref.shape[0]

    @pl.when(f == 0)
    def _():
        def rows(i, carry):
            r = pl.ds(pl.multiple_of(i * sub, sub), sub)
            h_scr[r, :] = _rms(x_ref[r, :], nw_ref[...]).astype(BF16)
            return carry
        lax.fori_loop(0, tm // sub, rows, 0)
        o_ref[...] = x_ref[...]

    h = h_scr[...]
    act = _silu(_dot(h, wg_ref[...])) * _dot(h, wu_ref[...])
    o_ref[...] += _dot(act.astype(BF16), wd_ref[...])

    if final:
        @pl.when(f == pl.num_programs(1) - 1)
        def _():
            def rows(i, carry):
                r = pl.ds(pl.multiple_of(i * sub, sub), sub)
                o_ref[r, :] = _rms(o_ref[r, :], fw_ref[...])
                return carry
            lax.fori_loop(0, tm // sub, rows, 0)


def _ffn(x, lw, final_w, *, tm, tf, final):
    rows, d = x.shape
    fh = lw["w_gate"].shape[1]
    act = pl.BlockSpec((tm, d), lambda m, f: (m, 0))
    vec = pl.BlockSpec((1, d), lambda m, f: (0, 0))
    wcol = pl.BlockSpec((d, tf), lambda m, f: (0, f))
    return pl.pallas_call(
        functools.partial(_ffn_body, sub=min(tm, 128), final=final),
        grid=(rows // tm, fh // tf),
        in_specs=[act, vec, wcol, wcol, pl.BlockSpec((tf, d), lambda m, f: (f, 0)), vec],
        out_specs=act,
        out_shape=jax.ShapeDtypeStruct((rows, d), F32),
        scratch_shapes=[pltpu.VMEM((tm, d), BF16)],
        compiler_params=_cparams("parallel", "arbitrary"),
        name="ffn",
    )(x, lw["norm_ffn_w"], lw["w_gate"], lw["w_up"], lw["w_down"], final_w)


def _tile(n, pref):
    t = min(n, pref)
    assert n % t == 0, (n, pref)
    return t


def _pad_rows(a, rows):
    a = jnp.swapaxes(a, 0, 1)
    return jnp.pad(a, ((0, 0), (0, rows - a.shape[1]), (0, 0)))


def kernel(x_prompt, x_sample, state_conv_a, state_ssm, state_conv_c, norm_mix_w, w_in, conv_a_w, conv_a_b,
           dt_bias, a_log, d_skip, ssm_norm_w, w_out_a, sgu_ln_w, sgu_ln_b, w_spatial, b_spatial, w_out_b,
           conv_c_w, w_out_c, w_o, norm_ffn_w, w_gate, w_up, w_down, norm_final_w):
    depth, d, _ = w_in.shape
    batch, seq, _ = x_prompt.shape
    sq, t_dec, _ = x_sample.shape
    I = ssm_norm_w.shape[1]
    H = dt_bias.shape[1]
    P, N = state_ssm.shape[3], state_ssm.shape[4]
    CD = conv_a_w.shape[2]
    G = (CD - I) // (2 * N)
    HG = H // G
    W = sgu_ln_w.shape[1]
    SG = w_spatial.shape[1]
    SCW = conv_c_w.shape[2]
    assert W == SCW and H <= LANES and LANES % P == 0 and (HG * P) % LANES == 0
    assert seq % CHUNK == 0 and t_dec <= SUBLANES and w_spatial.shape[2] == CHUNK

    sizes = (("z", I), ("x", I), ("bc", CD - I), ("u", W), ("v", W), ("scb", SCW), ("scc", SCW), ("sch", SCW),
             ("gate", 3 * d))
    off, o = {}, 0
    for name, s in sizes:
        off[name] = o
        o += s
    n_main = o
    cfg = dict(G=G, HG=HG, P=P, N=N, W=W, off=off)
    dt0 = I + CD

    def perm_w_in(w):
        main = jnp.concatenate([w[:, :dt0], w[:, dt0 + H:]], axis=1).astype(BF16)
        wdt = jnp.pad(w[:, dt0:dt0 + H], ((0, 0), (0, LANES - H)))
        hi = wdt.astype(BF16)
        lo = (wdt - hi.astype(F32)).astype(BF16)
        return main, hi, lo

    head_of_chan = jnp.arange(I) // P
    expand = (jnp.arange(LANES)[:, None] == head_of_chan[None, :]).astype(BF16)
    pos = (PAST_LEN + jnp.arange(t_dec)) % CHUNK
    padh = lambda v: jnp.pad(v, (0, LANES - H)).reshape(1, LANES)

    layers = []
    for i in range(depth):
        w_main, wdh, wdl = perm_w_in(w_in[i])
        w_sub = jnp.tril(w_spatial[i])[:, pos[:, None], pos[None, :]]
        layers.append(dict(
            norm_mix_w=norm_mix_w[i].reshape(1, d), w_in=w_main, wdh=wdh, wdl=wdl,
            conv_a_w=conv_a_w[i], conv_a_b=conv_a_b[i].reshape(1, CD),
            dt_bias=padh(dt_bias[i]), a_log=padh(a_log[i]),
            d_skip_e=jnp.repeat(d_skip[i], P).reshape(1, I), ssm_norm_w=ssm_norm_w[i].reshape(1, I),
            expand=expand,
            w_out_a=w_out_a[i].astype(BF16), w_out_b=w_out_b[i].astype(BF16), w_out_c=w_out_c[i].astype(BF16),
            w_o=w_o[i].astype(BF16),
            sgu_ln_w=sgu_ln_w[i].reshape(1, W), sgu_ln_b=sgu_ln_b[i].reshape(1, W),
            w_spatial=w_spatial[i],
            b_spatial_e=jnp.repeat(b_spatial[i].T, W // SG, axis=1),
            w_sub_e=jnp.repeat(jnp.transpose(w_sub, (1, 2, 0)).reshape(t_dec * t_dec, SG), W // SG, axis=1),
            b_sub_e=jnp.repeat(b_spatial[i][:, pos].T, W // SG, axis=1),
            conv_c_w=conv_c_w[i],
            norm_ffn_w=norm_ffn_w[i].reshape(1, d),
            w_gate=w_gate[i].astype(BF16), w_up=w_up[i].astype(BF16), w_down=w_down[i].astype(BF16),
        ))
    final_w = norm_final_w.reshape(1, d)

    rows_p = batch * seq
    rows_s = sq * t_dec
    yp = x_prompt.reshape(rows_p, d)
    ys = jnp.swapaxes(x_sample, 0, 1).reshape(rows_s, d)
    tn_in = _tile(n_main, 512)
    fh = w_gate.shape[2]
    tf = _tile(fh, 512)

    ca_p, ssm_p, cc_p, ca_s, ssm_s, cc_s, v_s = [], [], [], [], [], [], []
    for i, lw in enumerate(layers):
        last = i == depth - 1
        proj, dt_raw = _inproj(yp, lw["norm_mix_w"], lw["w_in"], lw["wdh"], lw["wdl"],
                               tm=_tile(rows_p, 1024), tn=tn_in)
        ya, cst_a, h_new = _ssd_prompt(proj, dt_raw, lw, cfg, batch, seq)
        yb, yc, cst_c = _mix_prompt(proj, lw, cfg, batch, seq, _tile(seq, 256))
        x1 = _merge(ya, yb, yc, proj, yp, lw, cfg, tm=_tile(rows_p, 512), tn=_tile(d, 256))
        yp = _ffn(x1, lw, final_w, tm=_tile(rows_p, 512), tf=tf, final=last)
        K = conv_a_w.shape[1]
        ca_p.append(cst_a[:, SUBLANES - (K - 1):, :])
        ssm_p.append(h_new.reshape(batch, H, P, N))
        cc_p.append(cst_c[:, SUBLANES - (conv_c_w.shape[1] - 1):, :])

        proj, dt_raw = _inproj(ys, lw["norm_mix_w"], lw["w_in"], lw["wdh"], lw["wdl"],
                               tm=_tile(rows_s, 512), tn=tn_in)
        proj3 = proj.reshape(t_dec, sq, n_main)
        dt3 = dt_raw.reshape(t_dec, sq, LANES)
        ypart, xdl, eae, bs, cs, cd, nx, nb, nc = _ssd_sample_a(
            proj3, dt3, jnp.swapaxes(state_conv_a[i], 0, 1), lw, cfg)
        ya_t, h_new = _ssd_sample_b(
            cd[:, :H].reshape(sq * H), state_ssm[i].reshape(sq, I, N),
            _pad_rows(cs, BF16_ROWS), _pad_rows(bs, BF16_ROWS), _pad_rows(xdl.astype(BF16), BF16_ROWS),
            _pad_rows(eae, SUBLANES), _pad_rows(ypart, SUBLANES), _pad_rows(proj3[:, :, :I], SUBLANES),
            lw["ssm_norm_w"], cfg, _tile(sq, 8))
        ya = jnp.swapaxes(ya_t[:, :t_dec, :], 0, 1).reshape(rows_s, I).astype(BF16)
        yb, yc, cst_c, v_norm = _mix_sample(proj3, jnp.swapaxes(state_conv_c[i], 0, 1), lw, cfg, _tile(W, 512))
        x1 = _merge(ya, yb.reshape(rows_s, W), yc.reshape(rows_s, SCW), proj, ys, lw, cfg,
                    tm=_tile(rows_s, 512), tn=_tile(d, 256))
        ys = _ffn(x1, lw, final_w, tm=_tile(rows_s, 512), tf=tf, final=last)
        ca_s.append(jnp.swapaxes(jnp.concatenate([nx, nb, nc], axis=-1), 0, 1))
        ssm_s.append(h_new.reshape(sq, H, P, N))
        cc_s.append(jnp.swapaxes(cst_c, 0, 1))
        v_s.append(jnp.swapaxes(v_norm, 0, 1))

    y_prompt = yp.reshape(batch, seq, d)
    y_sample = jnp.swapaxes(ys.reshape(t_dec, sq, d), 0, 1)
    return (y_prompt, y_sample, jnp.stack(ca_p), jnp.stack(ssm_p), jnp.stack(cc_p),
            jnp.stack(ca_s), jnp.stack(ssm_s), jnp.stack(cc_s), jnp.stack(v_s))
```

```python
import functools
import math

import jax
import jax.numpy as jnp
from jax import lax
from jax.experimental import pallas as pl
from jax.experimental.pallas import tpu as pltpu

F32 = jnp.float32
BF16 = jnp.bfloat16

RMS_EPS = 1e-6
LN_EPS = 1e-5
PAST_LEN = 16384
CHUNK = 128
LANES = 128
SUBLANES = 8
BF16_ROWS = 16
NEG_BIG = -1e30
VMEM_LIMIT_BYTES = 56 * 1024 * 1024


def _cparams(*sem):
    return pltpu.CompilerParams(dimension_semantics=sem, vmem_limit_bytes=VMEM_LIMIT_BYTES)


def _dot(a, b):
    return jnp.dot(a, b, preferred_element_type=F32)


def _dot_nt(a, b):
    return lax.dot_general(a, b, (((1,), (1,)), ((), ())), preferred_element_type=F32)


def _dot_tn(a, b):
    return lax.dot_general(a, b, (((0,), (0,)), ((), ())), preferred_element_type=F32)


def _split3(x):
    hi = x.astype(BF16)
    r = x - hi.astype(F32)
    mid = r.astype(BF16)
    lo = (r - mid.astype(F32)).astype(BF16)
    return hi, mid, lo


def _dot_exact_rhs(x, m_bf16):
    hi, mid, lo = _split3(x)
    return _dot(hi, m_bf16) + _dot(mid, m_bf16) + _dot(lo, m_bf16)


def _dot_exact_lhs(m_bf16, x):
    hi, mid, lo = _split3(x)
    return _dot(m_bf16, hi) + _dot(m_bf16, mid) + _dot(m_bf16, lo)


def _silu(x):
    return x * jax.nn.sigmoid(x)


def _softplus(x):
    return jnp.maximum(x, 0.0) + jnp.log1p(jnp.exp(-jnp.abs(x)))


def _rms(x, w):
    return x * lax.rsqrt(jnp.mean(x * x, axis=-1, keepdims=True) + RMS_EPS) * w


def _layernorm(x, w, b):
    mu = jnp.mean(x, axis=-1, keepdims=True)
    d = x - mu
    var = jnp.mean(d * d, axis=-1, keepdims=True)
    return d * lax.rsqrt(var + LN_EPS) * w + b


def _inproj_body(x_ref, nw_ref, w_ref, wdh_ref, wdl_ref, o_ref, dt_ref, h_scr, *, sub):
    tm = x_ref.shape[0]

    @pl.when(pl.program_id(1) == 0)
    def _():
        def rows(i, carry):
            r = pl.ds(pl.multiple_of(i * sub, sub), sub)
            h = _rms(x_ref[r, :], nw_ref[...])
            hb = h.astype(BF16)
            h_scr[r, :] = hb
            hl = (h - hb.astype(F32)).astype(BF16)
            dt_ref[r, :] = _dot(hb, wdh_ref[...]) + _dot(hb, wdl_ref[...]) + _dot(hl, wdh_ref[...])
            return carry
        lax.fori_loop(0, tm // sub, rows, 0)

    o_ref[...] = _dot(h_scr[...], w_ref[...])


def _prep_w_in_body(a_ref, b_ref, o_ref, *, j0, shift, sub):
    d = a_ref.shape[0]
    j = pl.program_id(1)

    @pl.when(j < j0)
    def _():
        def rows(i, carry):
            r = pl.ds(pl.multiple_of(i * sub, sub), sub)
            o_ref[r, :] = a_ref[r, :].astype(BF16)
            return carry
        lax.fori_loop(0, d // sub, rows, 0)

    @pl.when(j >= j0)
    def _():
        def rows(i, carry):
            r = pl.ds(pl.multiple_of(i * sub, sub), sub)
            o_ref[r, :] = jnp.concatenate([a_ref[r, shift:], b_ref[r, :shift]], axis=1).astype(BF16)
            return carry
        lax.fori_loop(0, d // sub, rows, 0)


def _prep_w_in(w_in, dt0, shift, n_main, tw):
    depth, d, _ = w_in.shape
    assert dt0 % tw == 0 and n_main % tw == 0 and shift < LANES and tw % LANES == 0
    j0 = dt0 // tw
    per = tw // LANES
    return pl.pallas_call(
        functools.partial(_prep_w_in_body, j0=j0, shift=shift, sub=min(d, 256)),
        grid=(depth, n_main // tw),
        in_specs=[
            pl.BlockSpec((None, d, tw), lambda l, j: (l, 0, j)),
            pl.BlockSpec((None, d, LANES), lambda l, j: (l, 0, jnp.where(j < j0, 0, per * (j + 1)))),
        ],
        out_specs=pl.BlockSpec((None, d, tw), lambda l, j: (l, 0, j)),
        out_shape=jax.ShapeDtypeStruct((depth, d, n_main), BF16),
        compiler_params=_cparams("parallel", "parallel"),
        name="prep_w_in",
    )(w_in, w_in)


def _inproj(x, nw, w, layer, wdh, wdl, *, tm, tn):
    rows, d = x.shape
    n = w.shape[2]
    return pl.pallas_call(
        functools.partial(_inproj_body, sub=min(tm, 128)),
        grid=(rows // tm, n // tn),
        in_specs=[
            pl.BlockSpec((tm, d), lambda m, j: (m, 0)),
            pl.BlockSpec((1, d), lambda m, j: (0, 0)),
            pl.BlockSpec((None, d, tn), lambda m, j: (layer, 0, j)),
            pl.BlockSpec((d, LANES), lambda m, j: (0, 0)),
            pl.BlockSpec((d, LANES), lambda m, j: (0, 0)),
        ],
        out_specs=[
            pl.BlockSpec((tm, tn), lambda m, j: (m, j)),
            pl.BlockSpec((tm, LANES), lambda m, j: (m, 0)),
        ],
        out_shape=[jax.ShapeDtypeStruct((rows, n), F32), jax.ShapeDtypeStruct((rows, LANES), F32)],
        scratch_shapes=[pltpu.VMEM((tm, d), BF16)],
        compiler_params=_cparams("parallel", "arbitrary"),
        name="inproj",
    )(x, nw, w, wdh, wdl)


def _ssd_prompt_body(z_ref, x_ref, bc_ref, dt_ref, cw_ref, cb_ref, dtb_ref, alog_ref, dsk_ref, nw_ref,
                     e_ref, ya_ref, cst_ref, h_ref, xpad, bcpad, xs_scr, bcs_scr, *, G, HG, P, N):
    c = pl.program_id(1)
    L = CHUNK
    GW = HG * P
    I = G * GW
    K = cw_ref.shape[0]
    T0 = SUBLANES - (K - 1)

    @pl.when(c == 0)
    def _():
        h_ref[...] = jnp.zeros(h_ref.shape, F32)
        xpad[0:SUBLANES, :] = jnp.zeros((SUBLANES, I), F32)
        bcpad[0:SUBLANES, :] = jnp.zeros((SUBLANES, 2 * G * N), F32)

    xpad[SUBLANES:SUBLANES + L, :] = x_ref[...]
    bcpad[SUBLANES:SUBLANES + L, :] = bc_ref[...]

    def conv(pad, lo, hi, col0):
        acc = pad[T0:T0 + L, lo:hi] * cw_ref[0:1, col0 + lo:col0 + hi]
        for j in range(1, K):
            acc = acc + pad[T0 + j:T0 + j + L, lo:hi] * cw_ref[j:j + 1, col0 + lo:col0 + hi]
        acc = acc + cb_ref[:, col0 + lo:col0 + hi]
        return _silu(acc)

    for g in range(G):
        xs_scr[:, g * GW:(g + 1) * GW] = conv(xpad, g * GW, (g + 1) * GW, 0)
    for g in range(2 * G):
        bcs_scr[:, g * N:(g + 1) * N] = conv(bcpad, g * N, (g + 1) * N, I)

    xpad[0:SUBLANES, :] = xpad[L:L + SUBLANES, :]
    bcpad[0:SUBLANES, :] = bcpad[L:L + SUBLANES, :]

    @pl.when(c == pl.num_programs(1) - 1)
    def _():
        cst_ref[0, :, 0:I] = xpad[L:L + SUBLANES, :]
        cst_ref[0, :, I:] = bcpad[L:L + SUBLANES, :]

    dt = _softplus(dt_ref[...] + dtb_ref[...])
    a = -jnp.exp(alog_ref[...])
    row = lax.broadcasted_iota(jnp.int32, (L, L), 0)
    col = lax.broadcasted_iota(jnp.int32, (L, L), 1)
    tri = row >= col
    tril_ones = jnp.where(tri, 1.0, 0.0).astype(BF16)
    acum = _dot_exact_lhs(tril_ones, dt * a)
    acum_t = acum.T
    last = acum[L - 1:L, :]
    dl = jnp.exp(last - acum)
    ea = jnp.exp(acum)
    cd_t = jnp.exp(acum_t[:, L - 1:L])
    dt_s, dl_s, ea_s = _split3(dt), _split3(dl), _split3(ea)
    lane = lax.broadcasted_iota(jnp.int32, (L, LANES), 1)
    HPL = LANES // P

    for g in range(G):
        gs = slice(g * GW, (g + 1) * GW)
        e_g = e_ref[:, gs]

        def expand(parts):
            return _dot(parts[0], e_g) + _dot(parts[1], e_g) + _dot(parts[2], e_g)

        xg = xs_scr[:, gs]
        xdt = xg * expand(dt_s)
        bb = bcs_scr[:, g * N:(g + 1) * N].astype(BF16)
        cb16 = bcs_scr[:, (G + g) * N:(G + g + 1) * N].astype(BF16)
        cbm = _dot_nt(cb16, bb)
        hg16 = h_ref[0, gs, :].astype(BF16)
        y = _dot_nt(cb16, hg16) * expand(ea_s) + dsk_ref[:, gs] * xg

        tiles = []
        for j in range(GW // LANES):
            xt = xdt[:, j * LANES:(j + 1) * LANES]
            acc = jnp.zeros((L, LANES), F32)
            for hh in range(HPL):
                h = g * HG + j * HPL + hh
                seg = acum[:, h:h + 1] - acum_t[h:h + 1, :]
                dec = jnp.exp(jnp.where(tri, seg, NEG_BIG))
                m = (cbm * dec).astype(BF16)
                inhead = (lane >= hh * P) & (lane < (hh + 1) * P)
                acc = acc + _dot(m, jnp.where(inhead, xt, 0.0).astype(BF16))
            tiles.append(acc)
        y = y + jnp.concatenate(tiles, axis=1)

        gt = y * _silu(z_ref[:, gs])
        ya_ref[:, gs] = _rms(gt, nw_ref[:, gs]).astype(BF16)

        s_new = _dot_tn((xdt * expand(dl_s)).astype(BF16), bb)
        for hh in range(HG):
            h = g * HG + hh
            rs = slice(g * GW + hh * P, g * GW + (hh + 1) * P)
            h_ref[0, rs, :] = h_ref[0, rs, :] * cd_t[h:h + 1, :] + s_new[hh * P:(hh + 1) * P, :]


def _ssd_prompt(proj, dt_raw, lw, cfg, batch, seq):
    G, HG, P, N = cfg["G"], cfg["HG"], cfg["P"], cfg["N"]
    I = G * HG * P
    BC = 2 * G * N
    CD = I + BC
    nc = seq // CHUNK
    off = cfg["off"]
    row = lambda b, c: b * nc + c
    full = lambda b, c: (0, 0)
    return pl.pallas_call(
        functools.partial(_ssd_prompt_body, G=G, HG=HG, P=P, N=N),
        grid=(batch, nc),
        in_specs=[
            pl.BlockSpec((CHUNK, I), lambda b, c: (row(b, c), off["z"] // I)),
            pl.BlockSpec((CHUNK, I), lambda b, c: (row(b, c), off["x"] // I)),
            pl.BlockSpec((CHUNK, BC), lambda b, c: (row(b, c), off["bc"] // BC)),
            pl.BlockSpec((CHUNK, LANES), lambda b, c: (row(b, c), 0)),
            pl.BlockSpec(lw["conv_a_w"].shape, full),
            pl.BlockSpec((1, CD), full),
            pl.BlockSpec((1, LANES), full),
            pl.BlockSpec((1, LANES), full),
            pl.BlockSpec((1, I), full),
            pl.BlockSpec((1, I), full),
            pl.BlockSpec((LANES, I), full),
        ],
        out_specs=[
            pl.BlockSpec((CHUNK, I), lambda b, c: (row(b, c), 0)),
            pl.BlockSpec((1, SUBLANES, CD), lambda b, c: (b, 0, 0)),
            pl.BlockSpec((1, I, N), lambda b, c: (b, 0, 0)),
        ],
        out_shape=[
            jax.ShapeDtypeStruct((batch * seq, I), BF16),
            jax.ShapeDtypeStruct((batch, SUBLANES, CD), F32),
            jax.ShapeDtypeStruct((batch, I, N), F32),
        ],
        scratch_shapes=[
            pltpu.VMEM((CHUNK + SUBLANES, I), F32),
            pltpu.VMEM((CHUNK + SUBLANES, BC), F32),
            pltpu.VMEM((CHUNK, I), F32),
            pltpu.VMEM((CHUNK, BC), F32),
        ],
        compiler_params=_cparams("parallel", "arbitrary"),
        name="ssd_prompt",
    )(proj, proj, proj, dt_raw, lw["conv_a_w"], lw["conv_a_b"], lw["dt_bias"], lw["a_log"],
      lw["d_skip_e"], lw["ssm_norm_w"], lw["expand"])


def _ssd_sample_a_body(x_ref, b_ref, c_ref, dt_ref, px_ref, pb_ref, pc_ref, cwx_ref, cwb_ref, cwc_ref,
                       cbx_ref, cbb_ref, cbc_ref, dtb_ref, alog_ref, dsk_ref, e_ref,
                       yp_ref, xdl_ref, ea_ref, bs_ref, cs_ref, cd_ref, nx_ref, nb_ref, nc_ref):
    T = x_ref.shape[0]
    K = cwx_ref.shape[0]

    def conv(p_ref, cur_ref, w_ref, bias_ref, new_ref):
        rows = [p_ref[j] for j in range(K - 1)] + [cur_ref[l] for l in range(T)]
        for j in range(K - 1):
            new_ref[j] = rows[T + j]
        outs = []
        for l in range(T):
            acc = rows[l] * w_ref[0:1, :]
            for j in range(1, K):
                acc = acc + rows[l + j] * w_ref[j:j + 1, :]
            outs.append(_silu(acc + bias_ref[...]))
        return outs

    xs = conv(px_ref, x_ref, cwx_ref, cbx_ref, nx_ref)
    bs = conv(pb_ref, b_ref, cwb_ref, cbb_ref, nb_ref)
    cs = conv(pc_ref, c_ref, cwc_ref, cbc_ref, nc_ref)

    a = -jnp.exp(alog_ref[...])
    dts, acums = [], []
    run = None
    for l in range(T):
        dt = _softplus(dt_ref[l] + dtb_ref[...])
        run = dt * a if run is None else run + dt * a
        dts.append(dt)
        acums.append(run)
    last = acums[T - 1]
    cd_ref[...] = jnp.exp(last)

    e_g = e_ref[...]

    def expand(coef):
        return _dot_exact_rhs(coef, e_g)

    xdts = [xs[l] * expand(dts[l]) for l in range(T)]
    for l in range(T):
        bs_ref[l] = bs[l].astype(BF16)
        cs_ref[l] = cs[l].astype(BF16)
        ea_ref[l] = expand(jnp.exp(acums[l]))
        xdl_ref[l] = xdts[l] * expand(jnp.exp(last - acums[l]))
        y = dsk_ref[...] * xs[l]
        for s in range(l + 1):
            cb = jnp.sum(cs[l] * bs[s], axis=-1, keepdims=True)
            y = y + expand(cb * jnp.exp(acums[l] - acums[s])) * xdts[s]
        yp_ref[l] = y


def _ssd_sample_a(proj3, dt3, conv_tm, lw, cfg):
    G, HG, P, N = cfg["G"], cfg["HG"], cfg["P"], cfg["N"]
    GW = HG * P
    I = G * GW
    T, SQ, _ = proj3.shape
    K = lw["conv_a_w"].shape[0]
    off = cfg["off"]
    xo, bo, co = off["x"] // GW, off["bc"] // N, off["bc"] // N + G
    sxo, sbo, sco = 0, I // N, I // N + G
    tok = lambda w, o: pl.BlockSpec((T, SQ, w), lambda g: (0, 0, o + g))
    pre = lambda w, o: pl.BlockSpec((K - 1, SQ, w), lambda g: (0, 0, o + g))
    par = lambda r, w, o: pl.BlockSpec((r, w), lambda g: (0, o + g))
    fixed = lambda shape: pl.BlockSpec(shape, lambda g: tuple(0 for _ in shape))
    return pl.pallas_call(
        _ssd_sample_a_body,
        grid=(G,),
        in_specs=[
            tok(GW, xo), tok(N, bo), tok(N, co), fixed((T, SQ, LANES)),
            pre(GW, sxo), pre(N, sbo), pre(N, sco),
            par(K, GW, sxo), par(K, N, sbo), par(K, N, sco),
            par(1, GW, sxo), par(1, N, sbo), par(1, N, sco),
            fixed((1, LANES)), fixed((1, LANES)), par(1, GW, 0), par(LANES, GW, 0),
        ],
        out_specs=[
            tok(GW, 0), tok(GW, 0), tok(GW, 0), tok(N, 0), tok(N, 0), fixed((SQ, LANES)),
            pre(GW, 0), pre(N, 0), pre(N, 0),
        ],
        out_shape=[
            jax.ShapeDtypeStruct((T, SQ, I), F32),
            jax.ShapeDtypeStruct((T, SQ, I), F32),
            jax.ShapeDtypeStruct((T, SQ, I), F32),
            jax.ShapeDtypeStruct((T, SQ, G * N), BF16),
            jax.ShapeDtypeStruct((T, SQ, G * N), BF16),
            jax.ShapeDtypeStruct((SQ, LANES), F32),
            jax.ShapeDtypeStruct((K - 1, SQ, I), F32),
            jax.ShapeDtypeStruct((K - 1, SQ, G * N), F32),
            jax.ShapeDtypeStruct((K - 1, SQ, G * N), F32),
        ],
        compiler_params=_cparams("arbitrary"),
        name="ssd_sample_a",
    )(proj3, proj3, proj3, dt3, conv_tm, conv_tm, conv_tm,
      lw["conv_a_w"], lw["conv_a_w"], lw["conv_a_w"], lw["conv_a_b"], lw["conv_a_b"], lw["conv_a_b"],
      lw["dt_bias"], lw["a_log"], lw["d_skip_e"], lw["expand"])


def _ssd_sample_b_body(cd_ref, h0_ref, cs_ref, bs_ref, xdl_ref, ea_ref, yp_ref, z_ref, nw_ref,
                       *rest, HG, P, H):
    ya_ref, hn_ref = rest[-2:]
    sb = pl.program_id(0)
    g = pl.program_id(1)
    SB = h0_ref.shape[0]
    TP = yp_ref.shape[1]
    for i in range(SB):
        hmat = h0_ref[i]
        yoff = _dot_nt(cs_ref[i], hmat.astype(BF16))
        y = yp_ref[i] + yoff[0:TP, :] * ea_ref[i]
        gt = y * _silu(z_ref[i])
        ya_ref[i] = _rms(gt, nw_ref[...])
        s_new = _dot_tn(xdl_ref[i], bs_ref[i])
        for hh in range(HG):
            cd = cd_ref[(sb * SB + i) * H + g * HG + hh]
            rs = slice(hh * P, (hh + 1) * P)
            hn_ref[i, rs, :] = hmat[rs, :] * cd + s_new[rs, :]


def _ssd_sample_b(cd_flat, h0_all, layer, h_acc, cs_t, bs_t, xdl_t, ea_t, yp_t, z_t, nw, cfg, sb):
    G, HG, P, N = cfg["G"], cfg["HG"], cfg["P"], cfg["N"]
    GW = HG * P
    _, SQ, I, _ = h0_all.shape
    TP = yp_t.shape[1]
    tok = lambda r, w: pl.BlockSpec((sb, r, w), lambda s, g: (s, 0, g))
    state = pl.BlockSpec((None, sb, GW, N), lambda s, g: (layer, s, g, 0))
    in_specs = [
        pl.BlockSpec(memory_space=pltpu.SMEM),
        state,
        tok(BF16_ROWS, N), tok(BF16_ROWS, N), tok(BF16_ROWS, GW),
        tok(TP, GW), tok(TP, GW), tok(TP, GW),
        pl.BlockSpec((1, GW), lambda s, g: (0, g)),
    ]
    args = [cd_flat, h0_all, cs_t, bs_t, xdl_t, ea_t, yp_t, z_t, nw]
    aliases = {}
    if h_acc is not None:
        in_specs.append(pl.BlockSpec(memory_space=pl.ANY))
        args.append(h_acc)
        aliases = {len(args) - 1: 1}
    return pl.pallas_call(
        functools.partial(_ssd_sample_b_body, HG=HG, P=P, H=G * HG),
        grid=(SQ // sb, G),
        in_specs=in_specs,
        out_specs=[tok(TP, GW), state],
        out_shape=[
            jax.ShapeDtypeStruct((SQ, TP, I), F32),
            jax.ShapeDtypeStruct(h0_all.shape, F32),
        ],
        input_output_aliases=aliases,
        compiler_params=_cparams("parallel", "parallel"),
        name="ssd_sample_b",
    )(*args)


def _mix_prompt_body(u_ref, v_ref, sb_ref, sc_ref, sh_ref, lnw_ref, lnb_ref, ws_ref, be_ref, cw_ref,
                     yb_ref, yc_ref, cst_ref, qpad, vb_scr):
    c = pl.program_id(1)
    R, W = u_ref.shape
    SG = ws_ref.shape[0]
    GWS = W // SG
    K = cw_ref.shape[0]
    T0 = SUBLANES - (K - 1)

    @pl.when(c == 0)
    def _():
        qpad[0:SUBLANES, :] = jnp.zeros((SUBLANES, W), F32)

    row = lax.broadcasted_iota(jnp.int32, (CHUNK, CHUNK), 0)
    col = lax.broadcasted_iota(jnp.int32, (CHUNK, CHUNK), 1)
    tri = row >= col
    for j in range(R // CHUNK):
        rs = slice(j * CHUNK, (j + 1) * CHUNK)
        v = _layernorm(jax.nn.gelu(v_ref[rs, :]), lnw_ref[...], lnb_ref[...])
        vb_scr[...] = v.astype(BF16)
        for g in range(SG):
            gs = slice(g * GWS, (g + 1) * GWS)
            w = jnp.where(tri, ws_ref[g], 0.0).astype(BF16)
            mixed = _dot(w, vb_scr[:, gs]) + be_ref[:, gs]
            yb_ref[rs, gs] = (jax.nn.gelu(u_ref[rs, gs]) * mixed).astype(BF16)

    qpad[SUBLANES:SUBLANES + R, :] = sc_ref[...] * sh_ref[...]
    conv = qpad[T0:T0 + R, :] * cw_ref[0:1, :]
    for j in range(1, K):
        conv = conv + qpad[T0 + j:T0 + j + R, :] * cw_ref[j:j + 1, :]
    yc_ref[...] = (sb_ref[...] * conv).astype(BF16)
    qpad[0:SUBLANES, :] = qpad[R:R + SUBLANES, :]

    @pl.when(c == pl.num_programs(1) - 1)
    def _():
        cst_ref[0] = qpad[R:R + SUBLANES, :]


def _mix_prompt(proj, lw, cfg, batch, seq, r):
    W = cfg["W"]
    off = cfg["off"]
    nr = seq // r
    tok = lambda name: pl.BlockSpec((r, W), lambda b, c: (b * nr + c, off[name] // W))
    fixed = lambda shape: pl.BlockSpec(shape, lambda b, c: tuple(0 for _ in shape))
    out = pl.BlockSpec((r, W), lambda b, c: (b * nr + c, 0))
    return pl.pallas_call(
        _mix_prompt_body,
        grid=(batch, nr),
        in_specs=[
            tok("u"), tok("v"), tok("scb"), tok("scc"), tok("sch"),
            fixed((1, W)), fixed((1, W)), fixed(lw["w_spatial"].shape), fixed((CHUNK, W)),
            fixed(lw["conv_c_w"].shape),
        ],
        out_specs=[out, out, pl.BlockSpec((1, SUBLANES, W), lambda b, c: (b, 0, 0))],
        out_shape=[
            jax.ShapeDtypeStruct((batch * seq, W), BF16),
            jax.ShapeDtypeStruct((batch * seq, W), BF16),
            jax.ShapeDtypeStruct((batch, SUBLANES, W), F32),
        ],
        scratch_shapes=[pltpu.VMEM((r + SUBLANES, W), F32), pltpu.VMEM((CHUNK, W), BF16)],
        compiler_params=_cparams("parallel", "arbitrary"),
        name="mix_prompt",
    )(proj, proj, proj, proj, proj, lw["sgu_ln_w"], lw["sgu_ln_b"], lw["w_spatial"], lw["b_spatial_e"],
      lw["conv_c_w"])


def _mix_sample_stats_body(v_ref, lnw_ref, lnb_ref, vo_ref):
    for l in range(v_ref.shape[0]):
        vo_ref[l] = _layernorm(jax.nn.gelu(v_ref[l]), lnw_ref[...], lnb_ref[...])


def _mix_sample_body(u_ref, v_ref, sb_ref, sc_ref, sh_ref, pq_ref, we_ref, be_ref, cw_ref,
                     yb_ref, yc_ref, cst_ref):
    T = u_ref.shape[0]
    K = cw_ref.shape[0]
    vs = [v_ref[l] for l in range(T)]
    for l in range(T):
        mixed = be_ref[l:l + 1, :]
        for s in range(T):
            mixed = mixed + we_ref[l * T + s:l * T + s + 1, :] * vs[s]
        yb_ref[l] = (jax.nn.gelu(u_ref[l]) * mixed).astype(BF16)
    rows = [pq_ref[j] for j in range(K - 1)] + [sc_ref[l] * sh_ref[l] for l in range(T)]
    for j in range(K - 1):
        cst_ref[j] = rows[T + j]
    for l in range(T):
        conv = rows[l] * cw_ref[0:1, :]
        for j in range(1, K):
            conv = conv + rows[l + j] * cw_ref[j:j + 1, :]
        yc_ref[l] = (sb_ref[l] * conv).astype(BF16)


def _mix_sample(proj3, convc_tm, lw, cfg, wc):
    W = cfg["W"]
    off = cfg["off"]
    T, SQ, _ = proj3.shape
    K = lw["conv_c_w"].shape[0]
    v_norm = pl.pallas_call(
        _mix_sample_stats_body,
        grid=(1,),
        in_specs=[
            pl.BlockSpec((T, SQ, W), lambda i: (0, 0, off["v"] // W)),
            pl.BlockSpec((1, W), lambda i: (0, 0)),
            pl.BlockSpec((1, W), lambda i: (0, 0)),
        ],
        out_specs=pl.BlockSpec((T, SQ, W), lambda i: (0, 0, 0)),
        out_shape=jax.ShapeDtypeStruct((T, SQ, W), F32),
        compiler_params=_cparams("arbitrary"),
        name="mix_sample_norm",
    )(proj3, lw["sgu_ln_w"], lw["sgu_ln_b"])
    tok = lambda name: pl.BlockSpec((T, SQ, wc), lambda j: (0, 0, off[name] // wc + j))
    loc = lambda r: pl.BlockSpec((r, SQ, wc), lambda j: (0, 0, j))
    par = lambda r: pl.BlockSpec((r, wc), lambda j: (0, j))
    yb, yc, cst = pl.pallas_call(
        _mix_sample_body,
        grid=(W // wc,),
        in_specs=[tok("u"), loc(T), tok("scb"), tok("scc"), tok("sch"), loc(K - 1),
                  par(T * T), par(T), par(K)],
        out_specs=[loc(T), loc(T), loc(K - 1)],
        out_shape=[
            jax.ShapeDtypeStruct((T, SQ, W), BF16),
            jax.ShapeDtypeStruct((T, SQ, W), BF16),
            jax.ShapeDtypeStruct((K - 1, SQ, W), F32),
        ],
        compiler_params=_cparams("parallel"),
        name="mix_sample",
    )(proj3, v_norm, proj3, proj3, proj3, convc_tm, lw["w_sub_e"], lw["b_sub_e"], lw["conv_c_w"])
    return yb, yc, cst, v_norm


def _merge_body(ya_ref, yb_ref, yc_ref, ga_ref, gb_ref, gc_ref, wa_ref, wb_ref, wc_ref, wo_ref, x_ref, o_ref):
    @pl.when(pl.program_id(1) == 0)
    def _():
        o_ref[...] = x_ref[...]

    merged = (jax.nn.sigmoid(ga_ref[...]) * _dot(ya_ref[...], wa_ref[...])
              + jax.nn.sigmoid(gb_ref[...]) * _dot(yb_ref[...], wb_ref[...])
              + jax.nn.sigmoid(gc_ref[...]) * _dot(yc_ref[...], wc_ref[...]))
    o_ref[...] += _dot(merged.astype(BF16), wo_ref[...])


def _merge(ya, yb, yc, proj, x, ws, layer, cfg, *, tm, tn):
    rows, d = x.shape
    I, W, SCW = ya.shape[1], yb.shape[1], yc.shape[1]
    go = cfg["off"]["gate"] // tn
    act = lambda w: pl.BlockSpec((tm, w), lambda m, j: (m, 0))
    gate = lambda k: pl.BlockSpec((tm, tn), lambda m, j: (m, go + k * (d // tn) + j))
    wcol = lambda k: pl.BlockSpec((None, k, tn), lambda m, j: (layer, 0, j))
    return pl.pallas_call(
        _merge_body,
        grid=(rows // tm, d // tn),
        in_specs=[act(I), act(W), act(SCW), gate(0), gate(1), gate(2),
                  wcol(I), wcol(W), wcol(SCW), pl.BlockSpec((None, tn, d), lambda m, j: (layer, j, 0)), act(d)],
        out_specs=act(d),
        out_shape=jax.ShapeDtypeStruct((rows, d), F32),
        compiler_params=_cparams("parallel", "arbitrary"),
        name="merge",
    )(ya, yb, yc, proj, proj, proj, ws["w_out_a"], ws["w_out_b"], ws["w_out_c"], ws["w_o"], x)


def _ffn_body(x_ref, nw_ref, wg_ref, wu_ref, wd_ref, fw_ref, o_ref, h_scr, *, sub, final):
    f = pl.program_id(1)
    tm = x_ref.shape[0]

    @pl.when(f == 0)
    def _():
        def rows(i, carry):
            r = pl.ds(pl.multiple_of(i * sub, sub), sub)
            h_scr[r, :] = _rms(x_ref[r, :], nw_ref[...]).astype(BF16)
            return carry
        lax.fori_loop(0, tm // sub, rows, 0)
        o_ref[...] = x_ref[...]

    h = h_scr[...]
    act = _silu(_dot(h, wg_ref[...])) * _dot(h, wu_ref[...])
    o_ref[...] += _dot(act.astype(BF16), wd_ref[...])

    if final:
        @pl.when(f == pl.num_programs(1) - 1)
        def _():
            def rows(i, carry):
                r = pl.ds(pl.multiple_of(i * sub, sub), sub)
                o_ref[r, :] = _rms(o_ref[r, :], fw_ref[...])
                return carry
            lax.fori_loop(0, tm // sub, rows, 0)


def _ffn(x, nw, ws, layer, final_w, *, tm, tf, final):
    rows, d = x.shape
    fh = ws["w_gate"].shape[2]
    act = pl.BlockSpec((tm, d), lambda m, f: (m, 0))
    vec = pl.BlockSpec((1, d), lambda m, f: (0, 0))
    wcol = pl.BlockSpec((None, d, tf), lambda m, f: (layer, 0, f))
    return pl.pallas_call(
        functools.partial(_ffn_body, sub=min(tm, 128), final=final),
        grid=(rows // tm, fh // tf),
        in_specs=[act, vec, wcol, wcol, pl.BlockSpec((None, tf, d), lambda m, f: (layer, f, 0)), vec],
        out_specs=act,
        out_shape=jax.ShapeDtypeStruct((rows, d), F32),
        scratch_shapes=[pltpu.VMEM((tm, d), BF16)],
        compiler_params=_cparams("parallel", "arbitrary"),
        name="ffn",
    )(x, nw, ws["w_gate"], ws["w_up"], ws["w_down"], final_w)


def _tile(n, pref, quantum=SUBLANES):
    for t in range(min(n, pref), 0, -1):
        if n % t == 0 and t % quantum == 0:
            return t
    raise ValueError((n, pref, quantum))


def _pad_rows(a, rows):
    a = jnp.swapaxes(a, 0, 1)
    return jnp.pad(a, ((0, 0), (0, rows - a.shape[1]), (0, 0)))


def kernel(x_prompt, x_sample, state_conv_a, state_ssm, state_conv_c, norm_mix_w, w_in, conv_a_w, conv_a_b,
           dt_bias, a_log, d_skip, ssm_norm_w, w_out_a, sgu_ln_w, sgu_ln_b, w_spatial, b_spatial, w_out_b,
           conv_c_w, w_out_c, w_o, norm_ffn_w, w_gate, w_up, w_down, norm_final_w):
    depth, d, _ = w_in.shape
    batch, seq, _ = x_prompt.shape
    sq, t_dec, _ = x_sample.shape
    I = ssm_norm_w.shape[1]
    H = dt_bias.shape[1]
    P, N = state_ssm.shape[3], state_ssm.shape[4]
    CD = conv_a_w.shape[2]
    G = (CD - I) // (2 * N)
    HG = H // G
    W = sgu_ln_w.shape[1]
    SG = w_spatial.shape[1]
    SCW = conv_c_w.shape[2]
    assert W == SCW and H <= LANES and LANES % P == 0 and (HG * P) % LANES == 0
    assert seq % CHUNK == 0 and t_dec <= SUBLANES and w_spatial.shape[2] == CHUNK

    sizes = (("z", I), ("x", I), ("bc", CD - I), ("u", W), ("v", W), ("scb", SCW), ("scc", SCW), ("sch", SCW),
             ("gate", 3 * d))
    off, o = {}, 0
    for name, s in sizes:
        off[name] = o
        o += s
    n_main = o
    cfg = dict(G=G, HG=HG, P=P, N=N, W=W, off=off)
    dt0 = I + CD

    w_in_main = _prep_w_in(w_in, dt0, H, n_main, _tile(math.gcd(dt0, n_main), 512, LANES))
    wdt = jnp.pad(w_in[:, :, dt0:dt0 + H], ((0, 0), (0, 0), (0, LANES - H)))
    wdt_hi = wdt.astype(BF16)
    wdt_lo = (wdt - wdt_hi.astype(F32)).astype(BF16)
    ws = dict(w_out_a=w_out_a.astype(BF16), w_out_b=w_out_b.astype(BF16), w_out_c=w_out_c.astype(BF16),
              w_o=w_o.astype(BF16), w_gate=w_gate.astype(BF16), w_up=w_up.astype(BF16),
              w_down=w_down.astype(BF16))

    head_of_chan = jnp.arange(I) // P
    expand = (jnp.arange(LANES)[:, None] == head_of_chan[None, :]).astype(BF16)
    pos = (PAST_LEN + jnp.arange(t_dec)) % CHUNK
    padh = lambda v: jnp.pad(v, (0, LANES - H)).reshape(1, LANES)

    layers = []
    for i in range(depth):
        w_sub = jnp.tril(w_spatial[i])[:, pos[:, None], pos[None, :]]
        layers.append(dict(
            norm_mix_w=norm_mix_w[i].reshape(1, d), wdh=wdt_hi[i], wdl=wdt_lo[i],
            conv_a_w=conv_a_w[i], conv_a_b=conv_a_b[i].reshape(1, CD),
            dt_bias=padh(dt_bias[i]), a_log=padh(a_log[i]),
            d_skip_e=jnp.repeat(d_skip[i], P).reshape(1, I), ssm_norm_w=ssm_norm_w[i].reshape(1, I),
            expand=expand,
            sgu_ln_w=sgu_ln_w[i].reshape(1, W), sgu_ln_b=sgu_ln_b[i].reshape(1, W),
            w_spatial=w_spatial[i],
            b_spatial_e=jnp.repeat(b_spatial[i].T, W // SG, axis=1),
            w_sub_e=jnp.repeat(jnp.transpose(w_sub, (1, 2, 0)).reshape(t_dec * t_dec, SG), W // SG, axis=1),
            b_sub_e=jnp.repeat(b_spatial[i][:, pos].T, W // SG, axis=1),
            conv_c_w=conv_c_w[i],
            norm_ffn_w=norm_ffn_w[i].reshape(1, d),
        ))
    final_w = norm_final_w.reshape(1, d)

    rows_p = batch * seq
    rows_s = sq * t_dec
    yp = x_prompt.reshape(rows_p, d)
    ys = jnp.swapaxes(x_sample, 0, 1).reshape(rows_s, d)
    tn_in = _tile(n_main, 1024, LANES)
    fh = w_gate.shape[2]
    tf = _tile(fh, 512, LANES)
    state_all = state_ssm.reshape(depth, sq, I, N)

    ca_p, ssm_p, cc_p, ca_s, cc_s, v_s = [], [], [], [], [], []
    h_acc = None
    for i, lw in enumerate(layers):
        last = i == depth - 1
        proj, dt_raw = _inproj(yp, lw["norm_mix_w"], w_in_main, i, lw["wdh"], lw["wdl"],
                               tm=_tile(rows_p, 1024), tn=tn_in)
        ya, cst_a, h_new = _ssd_prompt(proj, dt_raw, lw, cfg, batch, seq)
        yb, yc, cst_c = _mix_prompt(proj, lw, cfg, batch, seq, _tile(seq, 256, CHUNK))
        x1 = _merge(ya, yb, yc, proj, yp, ws, i, cfg, tm=_tile(rows_p, 512), tn=_tile(d, 256, LANES))
        yp = _ffn(x1, lw["norm_ffn_w"], ws, i, final_w, tm=_tile(rows_p, 512), tf=tf, final=last)
        K = conv_a_w.shape[1]
        ca_p.append(cst_a[:, SUBLANES - (K - 1):, :])
        ssm_p.append(h_new.reshape(batch, H, P, N))
        cc_p.append(cst_c[:, SUBLANES - (conv_c_w.shape[1] - 1):, :])

        proj, dt_raw = _inproj(ys, lw["norm_mix_w"], w_in_main, i, lw["wdh"], lw["wdl"],
                               tm=_tile(rows_s, 512), tn=tn_in)
        proj3 = proj.reshape(t_dec, sq, n_main)
        dt3 = dt_raw.reshape(t_dec, sq, LANES)
        ypart, xdl, eae, bs, cs, cd, nx, nb, nc = _ssd_sample_a(
            proj3, dt3, jnp.swapaxes(state_conv_a[i], 0, 1), lw, cfg)
        ya_t, h_acc = _ssd_sample_b(
            cd[:, :H].reshape(sq * H), state_all, i, h_acc,
            _pad_rows(cs, BF16_ROWS), _pad_rows(bs, BF16_ROWS), _pad_rows(xdl.astype(BF16), BF16_ROWS),
            _pad_rows(eae, SUBLANES), _pad_rows(ypart, SUBLANES), _pad_rows(proj3[:, :, :I], SUBLANES),
            lw["ssm_norm_w"], cfg, _tile(sq, 8))
        ya = jnp.swapaxes(ya_t[:, :t_dec, :], 0, 1).reshape(rows_s, I).astype(BF16)
        yb, yc, cst_c, v_norm = _mix_sample(proj3, jnp.swapaxes(state_conv_c[i], 0, 1), lw, cfg,
                                            _tile(W, 512, LANES))
        x1 = _merge(ya, yb.reshape(rows_s, W), yc.reshape(rows_s, SCW), proj, ys, ws, i, cfg,
                    tm=_tile(rows_s, 512), tn=_tile(d, 256, LANES))
        ys = _ffn(x1, lw["norm_ffn_w"], ws, i, final_w, tm=_tile(rows_s, 512), tf=tf, final=last)
        ca_s.append(jnp.swapaxes(jnp.concatenate([nx, nb, nc], axis=-1), 0, 1))
        cc_s.append(jnp.swapaxes(cst_c, 0, 1))
        v_s.append(jnp.swapaxes(v_norm, 0, 1))

    y_prompt = yp.reshape(batch, seq, d)
    y_sample = jnp.swapaxes(ys.reshape(t_dec, sq, d), 0, 1)
    return (y_prompt, y_sample, jnp.stack(ca_p), jnp.stack(ssm_p), jnp.stack(cc_p),
            jnp.stack(ca_s), h_acc.reshape(depth, sq, H, P, N), jnp.stack(cc_s), jnp.stack(v_s))
```

```python
import functools
import math

import jax
import jax.numpy as jnp
from jax import lax
from jax.experimental import pallas as pl
from jax.experimental.pallas import tpu as pltpu

F32 = jnp.float32
BF16 = jnp.bfloat16

RMS_EPS = 1e-6
LN_EPS = 1e-5
PAST_LEN = 16384
CHUNK = 128
LANES = 128
SUBLANES = 8
BF16_ROWS = 16
NEG_BIG = -1e30
VMEM_LIMIT_BYTES = 56 * 1024 * 1024


def _cparams(*sem):
    return pltpu.CompilerParams(dimension_semantics=sem, vmem_limit_bytes=VMEM_LIMIT_BYTES)


def _dot(a, b):
    return jnp.dot(a, b, preferred_element_type=F32)


def _dot_nt(a, b):
    return lax.dot_general(a, b, (((1,), (1,)), ((), ())), preferred_element_type=F32)


def _dot_tn(a, b):
    return lax.dot_general(a, b, (((0,), (0,)), ((), ())), preferred_element_type=F32)


def _split3(x):
    hi = x.astype(BF16)
    r = x - hi.astype(F32)
    mid = r.astype(BF16)
    lo = (r - mid.astype(F32)).astype(BF16)
    return hi, mid, lo


def _dot_exact_rhs(x, m_bf16):
    hi, mid, lo = _split3(x)
    return _dot(hi, m_bf16) + _dot(mid, m_bf16) + _dot(lo, m_bf16)


def _dot_exact_lhs(m_bf16, x):
    hi, mid, lo = _split3(x)
    return _dot(m_bf16, hi) + _dot(m_bf16, mid) + _dot(m_bf16, lo)


def _sigmoid(x):
    return 0.5 * jnp.tanh(0.5 * x) + 0.5


def _silu(x):
    return x * _sigmoid(x)


def _softplus(x):
    return jnp.maximum(x, 0.0) + jnp.log1p(jnp.exp(-jnp.abs(x)))


def _rms(x, w):
    return x * lax.rsqrt(jnp.mean(x * x, axis=-1, keepdims=True) + RMS_EPS) * w


def _layernorm(x, w, b):
    mu = jnp.mean(x, axis=-1, keepdims=True)
    d = x - mu
    var = jnp.mean(d * d, axis=-1, keepdims=True)
    return d * lax.rsqrt(var + LN_EPS) * w + b


def _inproj_body(x_ref, nw_ref, w_ref, wdh_ref, wdl_ref, o_ref, dt_ref, h_scr, *, sub):
    tm = x_ref.shape[0]

    @pl.when(pl.program_id(1) == 0)
    def _():
        def rows(i, carry):
            r = pl.ds(pl.multiple_of(i * sub, sub), sub)
            h = _rms(x_ref[r, :], nw_ref[...])
            hb = h.astype(BF16)
            h_scr[r, :] = hb
            hl = (h - hb.astype(F32)).astype(BF16)
            dt_ref[r, :] = _dot(hb, wdh_ref[...]) + _dot(hb, wdl_ref[...]) + _dot(hl, wdh_ref[...])
            return carry
        lax.fori_loop(0, tm // sub, rows, 0)

    o_ref[...] = _dot(h_scr[...], w_ref[...])


def _prep_w_in_body(a_ref, b_ref, o_ref, *, j0, shift, sub):
    tw, d = a_ref.shape
    j = pl.program_id(1)

    @pl.when(j < j0)
    def _():
        for k in range(d // sub):
            ks = slice(k * sub, (k + 1) * sub)
            o_ref[ks, :] = a_ref[:, ks].T.astype(BF16)

    @pl.when(j >= j0)
    def _():
        for k in range(d // sub):
            ks = slice(k * sub, (k + 1) * sub)
            src = jnp.concatenate([a_ref[shift:, ks], b_ref[:, ks]], axis=0)
            o_ref[ks, :] = src.T.astype(BF16)


def _prep_w_in(w_in_t, dt0, shift, n_main, tw):
    depth, _, d = w_in_t.shape
    assert dt0 % tw == 0 and n_main % tw == 0 and tw % shift == 0 and shift % SUBLANES == 0
    j0 = dt0 // tw
    per = tw // shift
    return pl.pallas_call(
        functools.partial(_prep_w_in_body, j0=j0, shift=shift, sub=min(d, 256)),
        grid=(depth, n_main // tw),
        in_specs=[
            pl.BlockSpec((None, tw, d), lambda l, j: (l, j, 0)),
            pl.BlockSpec((None, shift, d), lambda l, j: (l, jnp.where(j < j0, 0, per * (j + 1)), 0)),
        ],
        out_specs=pl.BlockSpec((None, d, tw), lambda l, j: (l, 0, j)),
        out_shape=jax.ShapeDtypeStruct((depth, d, n_main), BF16),
        compiler_params=_cparams("parallel", "parallel"),
        name="prep_w_in",
    )(w_in_t, w_in_t)


def _prep_w_dt_body(a_ref, hi_ref, lo_ref, *, sub):
    h, d = a_ref.shape
    for k in range(d // sub):
        ks = slice(k * sub, (k + 1) * sub)
        src = jnp.concatenate([a_ref[:, ks], jnp.zeros((LANES - h, sub), F32)], axis=0)
        w = src.T
        hi = w.astype(BF16)
        hi_ref[ks, :] = hi
        lo_ref[ks, :] = (w - hi.astype(F32)).astype(BF16)


def _prep_w_dt(w_in_t, dt0, h):
    depth, _, d = w_in_t.shape
    assert dt0 % h == 0 and h % SUBLANES == 0 and h <= LANES
    out = pl.BlockSpec((None, d, LANES), lambda l: (l, 0, 0))
    return pl.pallas_call(
        functools.partial(_prep_w_dt_body, sub=min(d, 256)),
        grid=(depth,),
        in_specs=[pl.BlockSpec((None, h, d), lambda l: (l, dt0 // h, 0))],
        out_specs=[out, out],
        out_shape=[jax.ShapeDtypeStruct((depth, d, LANES), BF16)] * 2,
        compiler_params=_cparams("parallel"),
        name="prep_w_dt",
    )(w_in_t)


def _inproj(x, nw, w, layer, wdh, wdl, *, tm, tn):
    rows, d = x.shape
    n = w.shape[2]
    return pl.pallas_call(
        functools.partial(_inproj_body, sub=min(tm, 128)),
        grid=(rows // tm, n // tn),
        in_specs=[
            pl.BlockSpec((tm, d), lambda m, j: (m, 0)),
            pl.BlockSpec((1, d), lambda m, j: (0, 0)),
            pl.BlockSpec((None, d, tn), lambda m, j: (layer, 0, j)),
            pl.BlockSpec((d, LANES), lambda m, j: (0, 0)),
            pl.BlockSpec((d, LANES), lambda m, j: (0, 0)),
        ],
        out_specs=[
            pl.BlockSpec((tm, tn), lambda m, j: (m, j)),
            pl.BlockSpec((tm, LANES), lambda m, j: (m, 0)),
        ],
        out_shape=[jax.ShapeDtypeStruct((rows, n), F32), jax.ShapeDtypeStruct((rows, LANES), F32)],
        scratch_shapes=[pltpu.VMEM((tm, d), BF16)],
        compiler_params=_cparams("parallel", "arbitrary"),
        name="inproj",
    )(x, nw, w, wdh, wdl)


def _ssd_prompt_body(z_ref, x_ref, bc_ref, dt_ref, cw_ref, cb_ref, dtb_ref, alog_ref, dsk_ref, nw_ref,
                     e_ref, ya_ref, cst_ref, h_ref, xpad, bcpad, *, G, HG, P, N):
    c = pl.program_id(1)
    L = CHUNK
    GW = HG * P
    I = G * GW
    K = cw_ref.shape[0]
    T0 = SUBLANES - (K - 1)

    @pl.when(c == 0)
    def _():
        h_ref[...] = jnp.zeros(h_ref.shape, F32)
        xpad[0:SUBLANES, :] = jnp.zeros((SUBLANES, I), F32)
        bcpad[0:SUBLANES, :] = jnp.zeros((SUBLANES, 2 * G * N), F32)

    xpad[SUBLANES:SUBLANES + L, :] = x_ref[...]
    bcpad[SUBLANES:SUBLANES + L, :] = bc_ref[...]

    def conv(pad, lo, hi, col0):
        acc = pad[T0:T0 + L, lo:hi] * cw_ref[0:1, col0 + lo:col0 + hi]
        for j in range(1, K):
            acc = acc + pad[T0 + j:T0 + j + L, lo:hi] * cw_ref[j:j + 1, col0 + lo:col0 + hi]
        acc = acc + cb_ref[:, col0 + lo:col0 + hi]
        return _silu(acc)

    dt = _softplus(dt_ref[...] + dtb_ref[...])
    a = -jnp.exp(alog_ref[...])
    row = lax.broadcasted_iota(jnp.int32, (L, L), 0)
    col = lax.broadcasted_iota(jnp.int32, (L, L), 1)
    tri = row >= col
    tril_ones = jnp.where(tri, 1.0, 0.0).astype(BF16)
    acum = _dot_exact_lhs(tril_ones, dt * a)
    acum_t = acum.T
    last = acum[L - 1:L, :]
    dl = jnp.exp(last - acum)
    ea = jnp.exp(acum)
    cd_t = jnp.exp(acum_t[:, L - 1:L])
    dt_s, dl_s, ea_s = _split3(dt), _split3(dl), _split3(ea)
    lane = lax.broadcasted_iota(jnp.int32, (L, LANES), 1)
    HPL = LANES // P

    for g in range(G):
        gs = slice(g * GW, (g + 1) * GW)
        e_g = e_ref[:, gs]

        def expand(parts):
            return _dot(parts[0], e_g) + _dot(parts[1], e_g) + _dot(parts[2], e_g)

        xg = conv(xpad, g * GW, (g + 1) * GW, 0)
        xdt = xg * expand(dt_s)
        bb = conv(bcpad, g * N, (g + 1) * N, I).astype(BF16)
        cb16 = conv(bcpad, (G + g) * N, (G + g + 1) * N, I).astype(BF16)
        cbm = _dot_nt(cb16, bb)
        hg16 = h_ref[0, gs, :].astype(BF16)
        y = _dot_nt(cb16, hg16) * expand(ea_s) + dsk_ref[:, gs] * xg

        tiles = []
        for j in range(GW // LANES):
            xt = xdt[:, j * LANES:(j + 1) * LANES]
            acc = jnp.zeros((L, LANES), F32)
            for hh in range(HPL):
                h = g * HG + j * HPL + hh
                seg = acum[:, h:h + 1] - acum_t[h:h + 1, :]
                dec = jnp.exp(jnp.where(tri, seg, NEG_BIG))
                m = (cbm * dec).astype(BF16)
                inhead = (lane >= hh * P) & (lane < (hh + 1) * P)
                acc = acc + _dot(m, jnp.where(inhead, xt, 0.0).astype(BF16))
            tiles.append(acc)
        y = y + jnp.concatenate(tiles, axis=1)

        gt = y * _silu(z_ref[:, gs])
        ya_ref[:, gs] = _rms(gt, nw_ref[:, gs]).astype(BF16)

        s_new = _dot_tn((xdt * expand(dl_s)).astype(BF16), bb)
        for hh in range(HG):
            h = g * HG + hh
            rs = slice(g * GW + hh * P, g * GW + (hh + 1) * P)
            h_ref[0, rs, :] = h_ref[0, rs, :] * cd_t[h:h + 1, :] + s_new[hh * P:(hh + 1) * P, :]

    xpad[0:SUBLANES, :] = xpad[L:L + SUBLANES, :]
    bcpad[0:SUBLANES, :] = bcpad[L:L + SUBLANES, :]

    @pl.when(c == pl.num_programs(1) - 1)
    def _():
        cst_ref[0, :, 0:I] = xpad[L:L + SUBLANES, :]
        cst_ref[0, :, I:] = bcpad[L:L + SUBLANES, :]


def _ssd_prompt(proj, dt_raw, lw, cfg, batch, seq):
    G, HG, P, N = cfg["G"], cfg["HG"], cfg["P"], cfg["N"]
    I = G * HG * P
    BC = 2 * G * N
    CD = I + BC
    nc = seq // CHUNK
    off = cfg["off"]
    row = lambda b, c: b * nc + c
    full = lambda b, c: (0, 0)
    return pl.pallas_call(
        functools.partial(_ssd_prompt_body, G=G, HG=HG, P=P, N=N),
        grid=(batch, nc),
        in_specs=[
            pl.BlockSpec((CHUNK, I), lambda b, c: (row(b, c), off["z"] // I)),
            pl.BlockSpec((CHUNK, I), lambda b, c: (row(b, c), off["x"] // I)),
            pl.BlockSpec((CHUNK, BC), lambda b, c: (row(b, c), off["bc"] // BC)),
            pl.BlockSpec((CHUNK, LANES), lambda b, c: (row(b, c), 0)),
            pl.BlockSpec(lw["conv_a_w"].shape, full),
            pl.BlockSpec((1, CD), full),
            pl.BlockSpec((1, LANES), full),
            pl.BlockSpec((1, LANES), full),
            pl.BlockSpec((1, I), full),
            pl.BlockSpec((1, I), full),
            pl.BlockSpec((LANES, I), full),
        ],
        out_specs=[
            pl.BlockSpec((CHUNK, I), lambda b, c: (row(b, c), 0)),
            pl.BlockSpec((1, SUBLANES, CD), lambda b, c: (b, 0, 0)),
            pl.BlockSpec((1, I, N), lambda b, c: (b, 0, 0)),
        ],
        out_shape=[
            jax.ShapeDtypeStruct((batch * seq, I), BF16),
            jax.ShapeDtypeStruct((batch, SUBLANES, CD), F32),
            jax.ShapeDtypeStruct((batch, I, N), F32),
        ],
        scratch_shapes=[
            pltpu.VMEM((CHUNK + SUBLANES, I), F32),
            pltpu.VMEM((CHUNK + SUBLANES, BC), F32),
        ],
        compiler_params=_cparams("parallel", "arbitrary"),
        name="ssd_prompt",
    )(proj, proj, proj, dt_raw, lw["conv_a_w"], lw["conv_a_b"], lw["dt_bias"], lw["a_log"],
      lw["d_skip_e"], lw["ssm_norm_w"], lw["expand"])


def _ssd_sample_a_body(x_ref, b_ref, c_ref, dt_ref, px_ref, pb_ref, pc_ref, cwx_ref, cwb_ref, cwc_ref,
                       cbx_ref, cbb_ref, cbc_ref, dtb_ref, alog_ref, dsk_ref, e_ref,
                       yp_ref, xdl_ref, ea_ref, bs_ref, cs_ref, cd_ref, nx_ref, nb_ref, nc_ref):
    T = x_ref.shape[0]
    K = cwx_ref.shape[0]

    def conv(p_ref, cur_ref, w_ref, bias_ref, new_ref):
        rows = [p_ref[j] for j in range(K - 1)] + [cur_ref[l] for l in range(T)]
        for j in range(K - 1):
            new_ref[j] = rows[T + j]
        outs = []
        for l in range(T):
            acc = rows[l] * w_ref[0:1, :]
            for j in range(1, K):
                acc = acc + rows[l + j] * w_ref[j:j + 1, :]
            outs.append(_silu(acc + bias_ref[...]))
        return outs

    xs = conv(px_ref, x_ref, cwx_ref, cbx_ref, nx_ref)
    bs = conv(pb_ref, b_ref, cwb_ref, cbb_ref, nb_ref)
    cs = conv(pc_ref, c_ref, cwc_ref, cbc_ref, nc_ref)

    a = -jnp.exp(alog_ref[...])
    dts, acums = [], []
    run = None
    for l in range(T):
        dt = _softplus(dt_ref[l] + dtb_ref[...])
        run = dt * a if run is None else run + dt * a
        dts.append(dt)
        acums.append(run)
    last = acums[T - 1]
    cd_ref[...] = jnp.exp(last)

    e_g = e_ref[...]

    def expand(coef):
        return _dot_exact_rhs(coef, e_g)

    xdts = [xs[l] * expand(dts[l]) for l in range(T)]
    for l in range(T):
        bs_ref[l] = bs[l].astype(BF16)
        cs_ref[l] = cs[l].astype(BF16)
        ea_ref[l] = expand(jnp.exp(acums[l]))
        xdl_ref[l] = xdts[l] * expand(jnp.exp(last - acums[l]))
        y = dsk_ref[...] * xs[l]
        for s in range(l + 1):
            cb = jnp.sum(cs[l] * bs[s], axis=-1, keepdims=True)
            y = y + expand(cb * jnp.exp(acums[l] - acums[s])) * xdts[s]
        yp_ref[l] = y


def _ssd_sample_a(proj3, dt3, conv_tm, lw, cfg):
    G, HG, P, N = cfg["G"], cfg["HG"], cfg["P"], cfg["N"]
    GW = HG * P
    I = G * GW
    T, SQ, _ = proj3.shape
    K = lw["conv_a_w"].shape[0]
    off = cfg["off"]
    xo, bo, co = off["x"] // GW, off["bc"] // N, off["bc"] // N + G
    sxo, sbo, sco = 0, I // N, I // N + G
    tok = lambda w, o: pl.BlockSpec((T, SQ, w), lambda g: (0, 0, o + g))
    pre = lambda w, o: pl.BlockSpec((K - 1, SQ, w), lambda g: (0, 0, o + g))
    par = lambda r, w, o: pl.BlockSpec((r, w), lambda g: (0, o + g))
    fixed = lambda shape: pl.BlockSpec(shape, lambda g: tuple(0 for _ in shape))
    return pl.pallas_call(
        _ssd_sample_a_body,
        grid=(G,),
        in_specs=[
            tok(GW, xo), tok(N, bo), tok(N, co), fixed((T, SQ, LANES)),
            pre(GW, sxo), pre(N, sbo), pre(N, sco),
            par(K, GW, sxo), par(K, N, sbo), par(K, N, sco),
            par(1, GW, sxo), par(1, N, sbo), par(1, N, sco),
            fixed((1, LANES)), fixed((1, LANES)), par(1, GW, 0), par(LANES, GW, 0),
        ],
        out_specs=[
            tok(GW, 0), tok(GW, 0), tok(GW, 0), tok(N, 0), tok(N, 0), fixed((SQ, LANES)),
            pre(GW, 0), pre(N, 0), pre(N, 0),
        ],
        out_shape=[
            jax.ShapeDtypeStruct((T, SQ, I), F32),
            jax.ShapeDtypeStruct((T, SQ, I), F32),
            jax.ShapeDtypeStruct((T, SQ, I), F32),
            jax.ShapeDtypeStruct((T, SQ, G * N), BF16),
            jax.ShapeDtypeStruct((T, SQ, G * N), BF16),
            jax.ShapeDtypeStruct((SQ, LANES), F32),
            jax.ShapeDtypeStruct((K - 1, SQ, I), F32),
            jax.ShapeDtypeStruct((K - 1, SQ, G * N), F32),
            jax.ShapeDtypeStruct((K - 1, SQ, G * N), F32),
        ],
        compiler_params=_cparams("arbitrary"),
        name="ssd_sample_a",
    )(proj3, proj3, proj3, dt3, conv_tm, conv_tm, conv_tm,
      lw["conv_a_w"], lw["conv_a_w"], lw["conv_a_w"], lw["conv_a_b"], lw["conv_a_b"], lw["conv_a_b"],
      lw["dt_bias"], lw["a_log"], lw["d_skip_e"], lw["expand"])


def _ssd_sample_b_body(cd_ref, h0_ref, cs_ref, bs_ref, xdl_ref, ea_ref, yp_ref, z_ref, nw_ref,
                       *rest, HG, P, H):
    ya_ref, hn_ref = rest[-2:]
    sb = pl.program_id(0)
    g = pl.program_id(1)
    SB = h0_ref.shape[0]
    TP = yp_ref.shape[1]
    for i in range(SB):
        hmat = h0_ref[i]
        yoff = _dot_nt(cs_ref[i], hmat.astype(BF16))
        y = yp_ref[i] + yoff[0:TP, :] * ea_ref[i]
        gt = y * _silu(z_ref[i])
        ya_ref[i] = _rms(gt, nw_ref[...])
        s_new = _dot_tn(xdl_ref[i], bs_ref[i])
        for hh in range(HG):
            cd = cd_ref[(sb * SB + i) * H + g * HG + hh]
            rs = slice(hh * P, (hh + 1) * P)
            hn_ref[i, rs, :] = hmat[rs, :] * cd + s_new[rs, :]


def _ssd_sample_b(cd_flat, h0_all, layer, h_acc, cs_t, bs_t, xdl_t, ea_t, yp_t, z_t, nw, cfg, sb):
    G, HG, P, N = cfg["G"], cfg["HG"], cfg["P"], cfg["N"]
    GW = HG * P
    _, SQ, I, _ = h0_all.shape
    TP = yp_t.shape[1]
    tok = lambda r, w: pl.BlockSpec((sb, r, w), lambda s, g: (s, 0, g))
    state = pl.BlockSpec((None, sb, GW, N), lambda s, g: (layer, s, g, 0))
    in_specs = [
        pl.BlockSpec(memory_space=pltpu.SMEM),
        state,
        tok(BF16_ROWS, N), tok(BF16_ROWS, N), tok(BF16_ROWS, GW),
        tok(TP, GW), tok(TP, GW), tok(TP, GW),
        pl.BlockSpec((1, GW), lambda s, g: (0, g)),
    ]
    args = [cd_flat, h0_all, cs_t, bs_t, xdl_t, ea_t, yp_t, z_t, nw]
    aliases = {}
    if h_acc is not None:
        in_specs.append(pl.BlockSpec(memory_space=pl.ANY))
        args.append(h_acc)
        aliases = {len(args) - 1: 1}
    return pl.pallas_call(
        functools.partial(_ssd_sample_b_body, HG=HG, P=P, H=G * HG),
        grid=(SQ // sb, G),
        in_specs=in_specs,
        out_specs=[tok(TP, GW), state],
        out_shape=[
            jax.ShapeDtypeStruct((SQ, TP, I), F32),
            jax.ShapeDtypeStruct(h0_all.shape, F32),
        ],
        input_output_aliases=aliases,
        compiler_params=_cparams("parallel", "parallel"),
        name="ssd_sample_b",
    )(*args)


def _mix_prompt_body(u_ref, v_ref, sb_ref, sc_ref, sh_ref, lnw_ref, lnb_ref, ws_ref, be_ref, cw_ref,
                     yb_ref, yc_ref, cst_ref, qpad, vb_scr):
    c = pl.program_id(1)
    R, W = u_ref.shape
    SG = ws_ref.shape[0]
    GWS = W // SG
    K = cw_ref.shape[0]
    T0 = SUBLANES - (K - 1)

    @pl.when(c == 0)
    def _():
        qpad[0:SUBLANES, :] = jnp.zeros((SUBLANES, W), F32)

    row = lax.broadcasted_iota(jnp.int32, (CHUNK, CHUNK), 0)
    col = lax.broadcasted_iota(jnp.int32, (CHUNK, CHUNK), 1)
    tri = row >= col
    for j in range(R // CHUNK):
        rs = slice(j * CHUNK, (j + 1) * CHUNK)
        v = _layernorm(jax.nn.gelu(v_ref[rs, :]), lnw_ref[...], lnb_ref[...])
        vb_scr[...] = v.astype(BF16)
        for g in range(SG):
            gs = slice(g * GWS, (g + 1) * GWS)
            w = jnp.where(tri, ws_ref[g], 0.0).astype(BF16)
            mixed = _dot(w, vb_scr[:, gs]) + be_ref[:, gs]
            yb_ref[rs, gs] = (jax.nn.gelu(u_ref[rs, gs]) * mixed).astype(BF16)

    qpad[SUBLANES:SUBLANES + R, :] = sc_ref[...] * sh_ref[...]
    conv = qpad[T0:T0 + R, :] * cw_ref[0:1, :]
    for j in range(1, K):
        conv = conv + qpad[T0 + j:T0 + j + R, :] * cw_ref[j:j + 1, :]
    yc_ref[...] = (sb_ref[...] * conv).astype(BF16)
    qpad[0:SUBLANES, :] = qpad[R:R + SUBLANES, :]

    @pl.when(c == pl.num_programs(1) - 1)
    def _():
        cst_ref[0] = qpad[R:R + SUBLANES, :]


def _mix_prompt(proj, lw, cfg, batch, seq, r):
    W = cfg["W"]
    off = cfg["off"]
    nr = seq // r
    tok = lambda name: pl.BlockSpec((r, W), lambda b, c: (b * nr + c, off[name] // W))
    fixed = lambda shape: pl.BlockSpec(shape, lambda b, c: tuple(0 for _ in shape))
    out = pl.BlockSpec((r, W), lambda b, c: (b * nr + c, 0))
    return pl.pallas_call(
        _mix_prompt_body,
        grid=(batch, nr),
        in_specs=[
            tok("u"), tok("v"), tok("scb"), tok("scc"), tok("sch"),
            fixed((1, W)), fixed((1, W)), fixed(lw["w_spatial"].shape), fixed((CHUNK, W)),
            fixed(lw["conv_c_w"].shape),
        ],
        out_specs=[out, out, pl.BlockSpec((1, SUBLANES, W), lambda b, c: (b, 0, 0))],
        out_shape=[
            jax.ShapeDtypeStruct((batch * seq, W), BF16),
            jax.ShapeDtypeStruct((batch * seq, W), BF16),
            jax.ShapeDtypeStruct((batch, SUBLANES, W), F32),
        ],
        scratch_shapes=[pltpu.VMEM((r + SUBLANES, W), F32), pltpu.VMEM((CHUNK, W), BF16)],
        compiler_params=_cparams("parallel", "arbitrary"),
        name="mix_prompt",
    )(proj, proj, proj, proj, proj, lw["sgu_ln_w"], lw["sgu_ln_b"], lw["w_spatial"], lw["b_spatial_e"],
      lw["conv_c_w"])


def _mix_sample_stats_body(v_ref, lnw_ref, lnb_ref, vo_ref):
    for l in range(v_ref.shape[0]):
        vo_ref[l] = _layernorm(jax.nn.gelu(v_ref[l]), lnw_ref[...], lnb_ref[...])


def _mix_sample_body(u_ref, v_ref, sb_ref, sc_ref, sh_ref, pq_ref, we_ref, be_ref, cw_ref,
                     yb_ref, yc_ref, cst_ref):
    T = u_ref.shape[0]
    K = cw_ref.shape[0]
    vs = [v_ref[l] for l in range(T)]
    for l in range(T):
        mixed = be_ref[l:l + 1, :]
        for s in range(T):
            mixed = mixed + we_ref[l * T + s:l * T + s + 1, :] * vs[s]
        yb_ref[l] = (jax.nn.gelu(u_ref[l]) * mixed).astype(BF16)
    rows = [pq_ref[j] for j in range(K - 1)] + [sc_ref[l] * sh_ref[l] for l in range(T)]
    for j in range(K - 1):
        cst_ref[j] = rows[T + j]
    for l in range(T):
        conv = rows[l] * cw_ref[0:1, :]
        for j in range(1, K):
            conv = conv + rows[l + j] * cw_ref[j:j + 1, :]
        yc_ref[l] = (sb_ref[l] * conv).astype(BF16)


def _mix_sample(proj3, convc_tm, lw, cfg, wc):
    W = cfg["W"]
    off = cfg["off"]
    T, SQ, _ = proj3.shape
    K = lw["conv_c_w"].shape[0]
    v_norm = pl.pallas_call(
        _mix_sample_stats_body,
        grid=(1,),
        in_specs=[
            pl.BlockSpec((T, SQ, W), lambda i: (0, 0, off["v"] // W)),
            pl.BlockSpec((1, W), lambda i: (0, 0)),
            pl.BlockSpec((1, W), lambda i: (0, 0)),
        ],
        out_specs=pl.BlockSpec((T, SQ, W), lambda i: (0, 0, 0)),
        out_shape=jax.ShapeDtypeStruct((T, SQ, W), F32),
        compiler_params=_cparams("arbitrary"),
        name="mix_sample_norm",
    )(proj3, lw["sgu_ln_w"], lw["sgu_ln_b"])
    tok = lambda name: pl.BlockSpec((T, SQ, wc), lambda j: (0, 0, off[name] // wc + j))
    loc = lambda r: pl.BlockSpec((r, SQ, wc), lambda j: (0, 0, j))
    par = lambda r: pl.BlockSpec((r, wc), lambda j: (0, j))
    yb, yc, cst = pl.pallas_call(
        _mix_sample_body,
        grid=(W // wc,),
        in_specs=[tok("u"), loc(T), tok("scb"), tok("scc"), tok("sch"), loc(K - 1),
                  par(T * T), par(T), par(K)],
        out_specs=[loc(T), loc(T), loc(K - 1)],
        out_shape=[
            jax.ShapeDtypeStruct((T, SQ, W), BF16),
            jax.ShapeDtypeStruct((T, SQ, W), BF16),
            jax.ShapeDtypeStruct((K - 1, SQ, W), F32),
        ],
        compiler_params=_cparams("parallel"),
        name="mix_sample",
    )(proj3, v_norm, proj3, proj3, proj3, convc_tm, lw["w_sub_e"], lw["b_sub_e"], lw["conv_c_w"])
    return yb, yc, cst, v_norm


def _merge_body(ya_ref, yb_ref, yc_ref, ga_ref, gb_ref, gc_ref, wa_ref, wb_ref, wc_ref, wo_ref, x_ref, o_ref):
    @pl.when(pl.program_id(1) == 0)
    def _():
        o_ref[...] = x_ref[...]

    merged = (_sigmoid(ga_ref[...]) * _dot(ya_ref[...], wa_ref[...])
              + _sigmoid(gb_ref[...]) * _dot(yb_ref[...], wb_ref[...])
              + _sigmoid(gc_ref[...]) * _dot(yc_ref[...], wc_ref[...]))
    o_ref[...] += _dot(merged.astype(BF16), wo_ref[...])


def _merge(ya, yb, yc, proj, x, ws, layer, cfg, *, tm, tn):
    rows, d = x.shape
    I, W, SCW = ya.shape[1], yb.shape[1], yc.shape[1]
    go = cfg["off"]["gate"] // tn
    act = lambda w: pl.BlockSpec((tm, w), lambda m, j: (m, 0))
    gate = lambda k: pl.BlockSpec((tm, tn), lambda m, j: (m, go + k * (d // tn) + j))
    wcol = lambda k: pl.BlockSpec((None, k, tn), lambda m, j: (layer, 0, j))
    return pl.pallas_call(
        _merge_body,
        grid=(rows // tm, d // tn),
        in_specs=[act(I), act(W), act(SCW), gate(0), gate(1), gate(2),
                  wcol(I), wcol(W), wcol(SCW), pl.BlockSpec((None, tn, d), lambda m, j: (layer, j, 0)), act(d)],
        out_specs=act(d),
        out_shape=jax.ShapeDtypeStruct((rows, d), F32),
        compiler_params=_cparams("parallel", "arbitrary"),
        name="merge",
    )(ya, yb, yc, proj, proj, proj, ws["w_out_a"], ws["w_out_b"], ws["w_out_c"], ws["w_o"], x)


def _ffn_body(x_ref, nw_ref, wg_ref, wu_ref, wd_ref, fw_ref, o_ref, h_scr, *, sub, final):
    f = pl.program_id(1)
    tm = x_ref.shape[0]

    @pl.when(f == 0)
    def _():
        def rows(i, carry):
            r = pl.ds(pl.multiple_of(i * sub, sub), sub)
            h_scr[r, :] = _rms(x_ref[r, :], nw_ref[...]).astype(BF16)
            return carry
        lax.fori_loop(0, tm // sub, rows, 0)
        o_ref[...] = x_ref[...]

    h = h_scr[...]
    act = _silu(_dot(h, wg_ref[...])) * _dot(h, wu_ref[...])
    o_ref[...] += _dot(act.astype(BF16), wd_ref[...])

    if final:
        @pl.when(f == pl.num_programs(1) - 1)
        def _():
            def rows(i, carry):
                r = pl.ds(pl.multiple_of(i * sub, sub), sub)
                o_ref[r, :] = _rms(o_ref[r, :], fw_ref[...])
                return carry
            lax.fori_loop(0, tm // sub, rows, 0)


def _ffn(x, nw, ws, layer, final_w, *, tm, tf, final):
    rows, d = x.shape
    fh = ws["w_gate"].shape[2]
    act = pl.BlockSpec((tm, d), lambda m, f: (m, 0))
    vec = pl.BlockSpec((1, d), lambda m, f: (0, 0))
    wcol = pl.BlockSpec((None, d, tf), lambda m, f: (layer, 0, f))
    return pl.pallas_call(
        functools.partial(_ffn_body, sub=min(tm, 128), final=final),
        grid=(rows // tm, fh // tf),
        in_specs=[act, vec, wcol, wcol, pl.BlockSpec((None, tf, d), lambda m, f: (layer, f, 0)), vec],
        out_specs=act,
        out_shape=jax.ShapeDtypeStruct((rows, d), F32),
        scratch_shapes=[pltpu.VMEM((tm, d), BF16)],
        compiler_params=_cparams("parallel", "arbitrary"),
        name="ffn",
    )(x, nw, ws["w_gate"], ws["w_up"], ws["w_down"], final_w)


def _tile(n, pref, quantum=SUBLANES):
    for t in range(min(n, pref), 0, -1):
        if n % t == 0 and t % quantum == 0:
            return t
    raise ValueError((n, pref, quantum))


def _pad_rows(a, rows):
    a = jnp.swapaxes(a, 0, 1)
    return jnp.pad(a, ((0, 0), (0, rows - a.shape[1]), (0, 0)))


def kernel(x_prompt, x_sample, state_conv_a, state_ssm, state_conv_c, norm_mix_w, w_in, conv_a_w, conv_a_b,
           dt_bias, a_log, d_skip, ssm_norm_w, w_out_a, sgu_ln_w, sgu_ln_b, w_spatial, b_spatial, w_out_b,
           conv_c_w, w_out_c, w_o, norm_ffn_w, w_gate, w_up, w_down, norm_final_w):
    depth, d, _ = w_in.shape
    batch, seq, _ = x_prompt.shape
    sq, t_dec, _ = x_sample.shape
    I = ssm_norm_w.shape[1]
    H = dt_bias.shape[1]
    P, N = state_ssm.shape[3], state_ssm.shape[4]
    CD = conv_a_w.shape[2]
    G = (CD - I) // (2 * N)
    HG = H // G
    W = sgu_ln_w.shape[1]
    SG = w_spatial.shape[1]
    SCW = conv_c_w.shape[2]
    assert W == SCW and H <= LANES and LANES % P == 0 and (HG * P) % LANES == 0
    assert seq % CHUNK == 0 and t_dec <= SUBLANES and w_spatial.shape[2] == CHUNK

    sizes = (("z", I), ("x", I), ("bc", CD - I), ("u", W), ("v", W), ("scb", SCW), ("scc", SCW), ("sch", SCW),
             ("gate", 3 * d))
    off, o = {}, 0
    for name, s in sizes:
        off[name] = o
        o += s
    n_main = o
    cfg = dict(G=G, HG=HG, P=P, N=N, W=W, off=off)
    dt0 = I + CD

    w_in_t = jnp.swapaxes(w_in, 1, 2)
    w_in_main = _prep_w_in(w_in_t, dt0, H, n_main, _tile(math.gcd(dt0, n_main), 512, LANES))
    wdt_hi, wdt_lo = _prep_w_dt(w_in_t, dt0, H)
    ws = dict(w_out_a=w_out_a.astype(BF16), w_out_b=w_out_b.astype(BF16), w_out_c=w_out_c.astype(BF16),
              w_o=w_o.astype(BF16), w_gate=w_gate.astype(BF16), w_up=w_up.astype(BF16),
              w_down=w_down.astype(BF16))

    head_of_chan = jnp.arange(I) // P
    expand = (jnp.arange(LANES)[:, None] == head_of_chan[None, :]).astype(BF16)
    pos = (PAST_LEN + jnp.arange(t_dec)) % CHUNK
    padh = lambda v: jnp.pad(v, (0, LANES - H)).reshape(1, LANES)

    layers = []
    for i in range(depth):
        w_sub = jnp.tril(w_spatial[i])[:, pos[:, None], pos[None, :]]
        layers.append(dict(
            norm_mix_w=norm_mix_w[i].reshape(1, d), wdh=wdt_hi[i], wdl=wdt_lo[i],
            conv_a_w=conv_a_w[i], conv_a_b=conv_a_b[i].reshape(1, CD),
            dt_bias=padh(dt_bias[i]), a_log=padh(a_log[i]),
            d_skip_e=jnp.repeat(d_skip[i], P).reshape(1, I), ssm_norm_w=ssm_norm_w[i].reshape(1, I),
            expand=expand,
            sgu_ln_w=sgu_ln_w[i].reshape(1, W), sgu_ln_b=sgu_ln_b[i].reshape(1, W),
            w_spatial=w_spatial[i],
            b_spatial_e=jnp.repeat(b_spatial[i].T, W // SG, axis=1),
            w_sub_e=jnp.repeat(jnp.transpose(w_sub, (1, 2, 0)).reshape(t_dec * t_dec, SG), W // SG, axis=1),
            b_sub_e=jnp.repeat(b_spatial[i][:, pos].T, W // SG, axis=1),
            conv_c_w=conv_c_w[i],
            norm_ffn_w=norm_ffn_w[i].reshape(1, d),
        ))
    final_w = norm_final_w.reshape(1, d)

    rows_p = batch * seq
    rows_s = sq * t_dec
    yp = x_prompt.reshape(rows_p, d)
    ys = jnp.swapaxes(x_sample, 0, 1).reshape(rows_s, d)
    tn_in = _tile(n_main, 1024, LANES)
    fh = w_gate.shape[2]
    tf = _tile(fh, 512, LANES)
    state_all = state_ssm.reshape(depth, sq, I, N)

    ca_p, ssm_p, cc_p, ca_s, cc_s, v_s = [], [], [], [], [], []
    h_acc = None
    for i, lw in enumerate(layers):
        last = i == depth - 1
        proj, dt_raw = _inproj(yp, lw["norm_mix_w"], w_in_main, i, lw["wdh"], lw["wdl"],
                               tm=_tile(rows_p, 1024), tn=tn_in)
        ya, cst_a, h_new = _ssd_prompt(proj, dt_raw, lw, cfg, batch, seq)
        yb, yc, cst_c = _mix_prompt(proj, lw, cfg, batch, seq, _tile(seq, 256, CHUNK))
        x1 = _merge(ya, yb, yc, proj, yp, ws, i, cfg, tm=_tile(rows_p, 512), tn=_tile(d, 256, LANES))
        yp = _ffn(x1, lw["norm_ffn_w"], ws, i, final_w, tm=_tile(rows_p, 512), tf=tf, final=last)
        K = conv_a_w.shape[1]
        ca_p.append(cst_a[:, SUBLANES - (K - 1):, :])
        ssm_p.append(h_new.reshape(batch, H, P, N))
        cc_p.append(cst_c[:, SUBLANES - (conv_c_w.shape[1] - 1):, :])

        proj, dt_raw = _inproj(ys, lw["norm_mix_w"], w_in_main, i, lw["wdh"], lw["wdl"],
                               tm=_tile(rows_s, 512), tn=tn_in)
        proj3 = proj.reshape(t_dec, sq, n_main)
        dt3 = dt_raw.reshape(t_dec, sq, LANES)
        ypart, xdl, eae, bs, cs, cd, nx, nb, nc = _ssd_sample_a(
            proj3, dt3, jnp.swapaxes(state_conv_a[i], 0, 1), lw, cfg)
        ya_t, h_acc = _ssd_sample_b(
            cd[:, :H].reshape(sq * H), state_all, i, h_acc,
            _pad_rows(cs, BF16_ROWS), _pad_rows(bs, BF16_ROWS), _pad_rows(xdl.astype(BF16), BF16_ROWS),
            _pad_rows(eae, SUBLANES), _pad_rows(ypart, SUBLANES), _pad_rows(proj3[:, :, :I], SUBLANES),
            lw["ssm_norm_w"], cfg, _tile(sq, 16))
        ya = jnp.swapaxes(ya_t[:, :t_dec, :], 0, 1).reshape(rows_s, I).astype(BF16)
        yb, yc, cst_c, v_norm = _mix_sample(proj3, jnp.swapaxes(state_conv_c[i], 0, 1), lw, cfg,
                                            _tile(W, 512, LANES))
        x1 = _merge(ya, yb.reshape(rows_s, W), yc.reshape(rows_s, SCW), proj, ys, ws, i, cfg,
                    tm=_tile(rows_s, 512), tn=_tile(d, 256, LANES))
        ys = _ffn(x1, lw["norm_ffn_w"], ws, i, final_w, tm=_tile(rows_s, 512), tf=tf, final=last)
        ca_s.append(jnp.swapaxes(jnp.concatenate([nx, nb, nc], axis=-1), 0, 1))
        cc_s.append(jnp.swapaxes(cst_c, 0, 1))
        v_s.append(jnp.swapaxes(v_norm, 0, 1))

    y_prompt = yp.reshape(batch, seq, d)
    y_sample = jnp.swapaxes(ys.reshape(t_dec, sq, d), 0, 1)
    return (y_prompt, y_sample, jnp.stack(ca_p), jnp.stack(ssm_p), jnp.stack(cc_p),
            jnp.stack(ca_s), h_acc.reshape(depth, sq, H, P, N), jnp.stack(cc_s), jnp.stack(v_s))
```

```python
import functools
import math

import jax
import jax.numpy as jnp
from jax import lax
from jax.experimental import pallas as pl
from jax.experimental.pallas import tpu as pltpu

F32 = jnp.float32
BF16 = jnp.bfloat16

RMS_EPS = 1e-6
LN_EPS = 1e-5
PAST_LEN = 16384
CHUNK = 128
LANES = 128
SUBLANES = 8
BF16_ROWS = 16
NEG_BIG = -1e30
VMEM_LIMIT_BYTES = 56 * 1024 * 1024


def _cparams(*sem):
    return pltpu.CompilerParams(dimension_semantics=sem, vmem_limit_bytes=VMEM_LIMIT_BYTES)


def _dot(a, b):
    return jnp.dot(a, b, preferred_element_type=F32)


def _dot_nt(a, b):
    return lax.dot_general(a, b, (((1,), (1,)), ((), ())), preferred_element_type=F32)


def _dot_tn(a, b):
    return lax.dot_general(a, b, (((0,), (0,)), ((), ())), preferred_element_type=F32)


def _split3(x):
    hi = x.astype(BF16)
    r = x - hi.astype(F32)
    mid = r.astype(BF16)
    lo = (r - mid.astype(F32)).astype(BF16)
    return hi, mid, lo


def _dot_exact_rhs(x, m_bf16):
    hi, mid, lo = _split3(x)
    return _dot(hi, m_bf16) + _dot(mid, m_bf16) + _dot(lo, m_bf16)


def _dot_exact_lhs(m_bf16, x):
    hi, mid, lo = _split3(x)
    return _dot(m_bf16, hi) + _dot(m_bf16, mid) + _dot(m_bf16, lo)


def _sigmoid(x):
    return 0.5 * jnp.tanh(0.5 * x) + 0.5


def _silu(x):
    return x * _sigmoid(x)


def _softplus(x):
    return jnp.maximum(x, 0.0) + jnp.log1p(jnp.exp(-jnp.abs(x)))


def _rms(x, w):
    return x * lax.rsqrt(jnp.mean(x * x, axis=-1, keepdims=True) + RMS_EPS) * w


def _causal_taps(v, w):
    k = w.shape[0]
    acc = v * w[0:1, :]
    for j in range(1, k):
        acc = pltpu.roll(acc, 1, axis=0) + v * w[j:j + 1, :]
    return acc[SUBLANES:, :]


def _layernorm(x, w, b):
    mu = jnp.mean(x, axis=-1, keepdims=True)
    d = x - mu
    var = jnp.mean(d * d, axis=-1, keepdims=True)
    return d * lax.rsqrt(var + LN_EPS) * w + b


def _inproj_body(x_ref, nw_ref, w_ref, wdh_ref, wdl_ref, o_ref, dt_ref, h_scr, *, sub):
    tm = x_ref.shape[0]

    @pl.when(pl.program_id(1) == 0)
    def _():
        def rows(i, carry):
            r = pl.ds(pl.multiple_of(i * sub, sub), sub)
            h = _rms(x_ref[r, :], nw_ref[...])
            hb = h.astype(BF16)
            h_scr[r, :] = hb
            hl = (h - hb.astype(F32)).astype(BF16)
            dt_ref[r, :] = _dot(hb, wdh_ref[...]) + _dot(hb, wdl_ref[...]) + _dot(hl, wdh_ref[...])
            return carry
        lax.fori_loop(0, tm // sub, rows, 0)

    o_ref[...] = _dot(h_scr[...], w_ref[...])


def _prep_w_in_body(a_ref, b_ref, o_ref, *, j0, shift, sub):
    tw, d = a_ref.shape
    j = pl.program_id(1)

    @pl.when(j < j0)
    def _():
        for k in range(d // sub):
            ks = slice(k * sub, (k + 1) * sub)
            o_ref[ks, :] = a_ref[:, ks].T.astype(BF16)

    @pl.when(j >= j0)
    def _():
        for k in range(d // sub):
            ks = slice(k * sub, (k + 1) * sub)
            src = jnp.concatenate([a_ref[shift:, ks], b_ref[:, ks]], axis=0)
            o_ref[ks, :] = src.T.astype(BF16)


def _prep_w_in(w_in_t, dt0, shift, n_main, tw):
    depth, _, d = w_in_t.shape
    assert dt0 % tw == 0 and n_main % tw == 0 and tw % shift == 0 and shift % SUBLANES == 0
    j0 = dt0 // tw
    per = tw // shift
    return pl.pallas_call(
        functools.partial(_prep_w_in_body, j0=j0, shift=shift, sub=min(d, 256)),
        grid=(depth, n_main // tw),
        in_specs=[
            pl.BlockSpec((None, tw, d), lambda l, j: (l, j, 0)),
            pl.BlockSpec((None, shift, d), lambda l, j: (l, jnp.where(j < j0, 0, per * (j + 1)), 0)),
        ],
        out_specs=pl.BlockSpec((None, d, tw), lambda l, j: (l, 0, j)),
        out_shape=jax.ShapeDtypeStruct((depth, d, n_main), BF16),
        compiler_params=_cparams("parallel", "parallel"),
        name="prep_w_in",
    )(w_in_t, w_in_t)


def _prep_w_dt_body(a_ref, hi_ref, lo_ref, *, sub):
    h, d = a_ref.shape
    for k in range(d // sub):
        ks = slice(k * sub, (k + 1) * sub)
        src = jnp.concatenate([a_ref[:, ks], jnp.zeros((LANES - h, sub), F32)], axis=0)
        w = src.T
        hi = w.astype(BF16)
        hi_ref[ks, :] = hi
        lo_ref[ks, :] = (w - hi.astype(F32)).astype(BF16)


def _prep_w_dt(w_in_t, dt0, h):
    depth, _, d = w_in_t.shape
    assert dt0 % h == 0 and h % SUBLANES == 0 and h <= LANES
    out = pl.BlockSpec((None, d, LANES), lambda l: (l, 0, 0))
    return pl.pallas_call(
        functools.partial(_prep_w_dt_body, sub=min(d, 256)),
        grid=(depth,),
        in_specs=[pl.BlockSpec((None, h, d), lambda l: (l, dt0 // h, 0))],
        out_specs=[out, out],
        out_shape=[jax.ShapeDtypeStruct((depth, d, LANES), BF16)] * 2,
        compiler_params=_cparams("parallel"),
        name="prep_w_dt",
    )(w_in_t)


def _inproj(x, nw, w, layer, wdh, wdl, *, tm, tn):
    rows, d = x.shape
    n = w.shape[2]
    return pl.pallas_call(
        functools.partial(_inproj_body, sub=min(tm, 128)),
        grid=(rows // tm, n // tn),
        in_specs=[
            pl.BlockSpec((tm, d), lambda m, j: (m, 0)),
            pl.BlockSpec((1, d), lambda m, j: (0, 0)),
            pl.BlockSpec((None, d, tn), lambda m, j: (layer, 0, j)),
            pl.BlockSpec((d, LANES), lambda m, j: (0, 0)),
            pl.BlockSpec((d, LANES), lambda m, j: (0, 0)),
        ],
        out_specs=[
            pl.BlockSpec((tm, tn), lambda m, j: (m, j)),
            pl.BlockSpec((tm, LANES), lambda m, j: (m, 0)),
        ],
        out_shape=[jax.ShapeDtypeStruct((rows, n), F32), jax.ShapeDtypeStruct((rows, LANES), F32)],
        scratch_shapes=[pltpu.VMEM((tm, d), BF16)],
        compiler_params=_cparams("parallel", "arbitrary"),
        name="inproj",
    )(x, nw, w, wdh, wdl)


def _ssd_prompt_body(z_ref, x_ref, bc_ref, dt_ref, cw_ref, cb_ref, dtb_ref, alog_ref, dsk_ref, nw_ref,
                     e_ref, ya_ref, cst_ref, h_ref, xpad, bcpad, *, G, HG, P, N):
    c = pl.program_id(1)
    L = CHUNK
    GW = HG * P
    I = G * GW

    @pl.when(c == 0)
    def _():
        h_ref[...] = jnp.zeros(h_ref.shape, F32)
        xpad[0:SUBLANES, :] = jnp.zeros((SUBLANES, I), F32)
        bcpad[0:SUBLANES, :] = jnp.zeros((SUBLANES, 2 * G * N), F32)

    xpad[SUBLANES:SUBLANES + L, :] = x_ref[...]
    bcpad[SUBLANES:SUBLANES + L, :] = bc_ref[...]

    def conv(pad, lo, hi, col0):
        acc = _causal_taps(pad[:, lo:hi], cw_ref[:, col0 + lo:col0 + hi])
        return _silu(acc + cb_ref[:, col0 + lo:col0 + hi])

    dt = _softplus(dt_ref[...] + dtb_ref[...])
    a = -jnp.exp(alog_ref[...])
    row = lax.broadcasted_iota(jnp.int32, (L, L), 0)
    col = lax.broadcasted_iota(jnp.int32, (L, L), 1)
    tri = row >= col
    tril_ones = jnp.where(tri, 1.0, 0.0).astype(BF16)
    acum = _dot_exact_lhs(tril_ones, dt * a)
    acum_t = acum.T
    last = acum[L - 1:L, :]
    dl = jnp.exp(last - acum)
    ea = jnp.exp(acum)
    cd_t = jnp.exp(acum_t[:, L - 1:L])
    dt_s, dl_s, ea_s = _split3(dt), _split3(dl), _split3(ea)
    lane = lax.broadcasted_iota(jnp.int32, (L, LANES), 1)
    HPL = LANES // P

    for g in range(G):
        gs = slice(g * GW, (g + 1) * GW)
        e_g = e_ref[:, gs]

        def expand(parts):
            return _dot(parts[0], e_g) + _dot(parts[1], e_g) + _dot(parts[2], e_g)

        xg = conv(xpad, g * GW, (g + 1) * GW, 0)
        xdt = xg * expand(dt_s)
        bb = conv(bcpad, g * N, (g + 1) * N, I).astype(BF16)
        cb16 = conv(bcpad, (G + g) * N, (G + g + 1) * N, I).astype(BF16)
        cbm = _dot_nt(cb16, bb)
        hg16 = h_ref[0, gs, :].astype(BF16)
        y = _dot_nt(cb16, hg16) * expand(ea_s) + dsk_ref[:, gs] * xg

        tiles = []
        for j in range(GW // LANES):
            xt = xdt[:, j * LANES:(j + 1) * LANES]
            acc = jnp.zeros((L, LANES), F32)
            for hh in range(HPL):
                h = g * HG + j * HPL + hh
                seg = acum[:, h:h + 1] - acum_t[h:h + 1, :]
                dec = jnp.exp(jnp.where(tri, seg, NEG_BIG))
                m = (cbm * dec).astype(BF16)
                inhead = (lane >= hh * P) & (lane < (hh + 1) * P)
                acc = acc + _dot(m, jnp.where(inhead, xt, 0.0).astype(BF16))
            tiles.append(acc)
        y = y + jnp.concatenate(tiles, axis=1)

        gt = y * _silu(z_ref[:, gs])
        ya_ref[:, gs] = _rms(gt, nw_ref[:, gs]).astype(BF16)

        s_new = _dot_tn((xdt * expand(dl_s)).astype(BF16), bb)
        for hh in range(HG):
            h = g * HG + hh
            rs = slice(g * GW + hh * P, g * GW + (hh + 1) * P)
            h_ref[0, rs, :] = h_ref[0, rs, :] * cd_t[h:h + 1, :] + s_new[hh * P:(hh + 1) * P, :]

    xpad[0:SUBLANES, :] = xpad[L:L + SUBLANES, :]
    bcpad[0:SUBLANES, :] = bcpad[L:L + SUBLANES, :]

    @pl.when(c == pl.num_programs(1) - 1)
    def _():
        cst_ref[0, :, 0:I] = xpad[L:L + SUBLANES, :]
        cst_ref[0, :, I:] = bcpad[L:L + SUBLANES, :]


def _ssd_prompt(proj, dt_raw, lw, cfg, batch, seq):
    G, HG, P, N = cfg["G"], cfg["HG"], cfg["P"], cfg["N"]
    I = G * HG * P
    BC = 2 * G * N
    CD = I + BC
    nc = seq // CHUNK
    off = cfg["off"]
    row = lambda b, c: b * nc + c
    full = lambda b, c: (0, 0)
    return pl.pallas_call(
        functools.partial(_ssd_prompt_body, G=G, HG=HG, P=P, N=N),
        grid=(batch, nc),
        in_specs=[
            pl.BlockSpec((CHUNK, I), lambda b, c: (row(b, c), off["z"] // I)),
            pl.BlockSpec((CHUNK, I), lambda b, c: (row(b, c), off["x"] // I)),
            pl.BlockSpec((CHUNK, BC), lambda b, c: (row(b, c), off["bc"] // BC)),
            pl.BlockSpec((CHUNK, LANES), lambda b, c: (row(b, c), 0)),
            pl.BlockSpec(lw["conv_a_w"].shape, full),
            pl.BlockSpec((1, CD), full),
            pl.BlockSpec((1, LANES), full),
            pl.BlockSpec((1, LANES), full),
            pl.BlockSpec((1, I), full),
            pl.BlockSpec((1, I), full),
            pl.BlockSpec((LANES, I), full),
        ],
        out_specs=[
            pl.BlockSpec((CHUNK, I), lambda b, c: (row(b, c), 0)),
            pl.BlockSpec((1, SUBLANES, CD), lambda b, c: (b, 0, 0)),
            pl.BlockSpec((1, I, N), lambda b, c: (b, 0, 0)),
        ],
        out_shape=[
            jax.ShapeDtypeStruct((batch * seq, I), BF16),
            jax.ShapeDtypeStruct((batch, SUBLANES, CD), F32),
            jax.ShapeDtypeStruct((batch, I, N), F32),
        ],
        scratch_shapes=[
            pltpu.VMEM((CHUNK + SUBLANES, I), F32),
            pltpu.VMEM((CHUNK + SUBLANES, BC), F32),
        ],
        compiler_params=_cparams("parallel", "arbitrary"),
        name="ssd_prompt",
    )(proj, proj, proj, dt_raw, lw["conv_a_w"], lw["conv_a_b"], lw["dt_bias"], lw["a_log"],
      lw["d_skip_e"], lw["ssm_norm_w"], lw["expand"])


def _ssd_sample_a_body(x_ref, b_ref, c_ref, dt_ref, px_ref, pb_ref, pc_ref, cwx_ref, cwb_ref, cwc_ref,
                       cbx_ref, cbb_ref, cbc_ref, dtb_ref, alog_ref, dsk_ref, e_ref,
                       yp_ref, xdl_ref, ea_ref, bs_ref, cs_ref, cd_ref, nx_ref, nb_ref, nc_ref):
    T = x_ref.shape[0]
    K = cwx_ref.shape[0]

    def conv(p_ref, cur_ref, w_ref, bias_ref, new_ref):
        rows = [p_ref[j] for j in range(K - 1)] + [cur_ref[l] for l in range(T)]
        for j in range(K - 1):
            new_ref[j] = rows[T + j]
        outs = []
        for l in range(T):
            acc = rows[l] * w_ref[0:1, :]
            for j in range(1, K):
                acc = acc + rows[l + j] * w_ref[j:j + 1, :]
            outs.append(_silu(acc + bias_ref[...]))
        return outs

    xs = conv(px_ref, x_ref, cwx_ref, cbx_ref, nx_ref)
    bs = conv(pb_ref, b_ref, cwb_ref, cbb_ref, nb_ref)
    cs = conv(pc_ref, c_ref, cwc_ref, cbc_ref, nc_ref)

    a = -jnp.exp(alog_ref[...])
    dts, acums = [], []
    run = None
    for l in range(T):
        dt = _softplus(dt_ref[l] + dtb_ref[...])
        run = dt * a if run is None else run + dt * a
        dts.append(dt)
        acums.append(run)
    last = acums[T - 1]
    cd_ref[...] = jnp.exp(last)

    e_g = e_ref[...]

    def expand(coef):
        return _dot_exact_rhs(coef, e_g)

    xdts = [xs[l] * expand(dts[l]) for l in range(T)]
    for l in range(T):
        bs_ref[l] = bs[l].astype(BF16)
        cs_ref[l] = cs[l].astype(BF16)
        ea_ref[l] = expand(jnp.exp(acums[l]))
        xdl_ref[l] = xdts[l] * expand(jnp.exp(last - acums[l]))
        y = dsk_ref[...] * xs[l]
        for s in range(l + 1):
            cb = jnp.sum(cs[l] * bs[s], axis=-1, keepdims=True)
            y = y + expand(cb * jnp.exp(acums[l] - acums[s])) * xdts[s]
        yp_ref[l] = y


def _ssd_sample_a(proj3, dt3, conv_tm, lw, cfg):
    G, HG, P, N = cfg["G"], cfg["HG"], cfg["P"], cfg["N"]
    GW = HG * P
    I = G * GW
    T, SQ, _ = proj3.shape
    K = lw["conv_a_w"].shape[0]
    off = cfg["off"]
    xo, bo, co = off["x"] // GW, off["bc"] // N, off["bc"] // N + G
    sxo, sbo, sco = 0, I // N, I // N + G
    tok = lambda w, o: pl.BlockSpec((T, SQ, w), lambda g: (0, 0, o + g))
    pre = lambda w, o: pl.BlockSpec((K - 1, SQ, w), lambda g: (0, 0, o + g))
    par = lambda r, w, o: pl.BlockSpec((r, w), lambda g: (0, o + g))
    fixed = lambda shape: pl.BlockSpec(shape, lambda g: tuple(0 for _ in shape))
    return pl.pallas_call(
        _ssd_sample_a_body,
        grid=(G,),
        in_specs=[
            tok(GW, xo), tok(N, bo), tok(N, co), fixed((T, SQ, LANES)),
            pre(GW, sxo), pre(N, sbo), pre(N, sco),
            par(K, GW, sxo), par(K, N, sbo), par(K, N, sco),
            par(1, GW, sxo), par(1, N, sbo), par(1, N, sco),
            fixed((1, LANES)), fixed((1, LANES)), par(1, GW, 0), par(LANES, GW, 0),
        ],
        out_specs=[
            tok(GW, 0), tok(GW, 0), tok(GW, 0), tok(N, 0), tok(N, 0), fixed((SQ, LANES)),
            pre(GW, 0), pre(N, 0), pre(N, 0),
        ],
        out_shape=[
            jax.ShapeDtypeStruct((T, SQ, I), F32),
            jax.ShapeDtypeStruct((T, SQ, I), F32),
            jax.ShapeDtypeStruct((T, SQ, I), F32),
            jax.ShapeDtypeStruct((T, SQ, G * N), BF16),
            jax.ShapeDtypeStruct((T, SQ, G * N), BF16),
            jax.ShapeDtypeStruct((SQ, LANES), F32),
            jax.ShapeDtypeStruct((K - 1, SQ, I), F32),
            jax.ShapeDtypeStruct((K - 1, SQ, G * N), F32),
            jax.ShapeDtypeStruct((K - 1, SQ, G * N), F32),
        ],
        compiler_params=_cparams("arbitrary"),
        name="ssd_sample_a",
    )(proj3, proj3, proj3, dt3, conv_tm, conv_tm, conv_tm,
      lw["conv_a_w"], lw["conv_a_w"], lw["conv_a_w"], lw["conv_a_b"], lw["conv_a_b"], lw["conv_a_b"],
      lw["dt_bias"], lw["a_log"], lw["d_skip_e"], lw["expand"])


def _ssd_sample_b_body(cd_ref, h0_ref, cs_ref, bs_ref, xdl_ref, ea_ref, yp_ref, z_ref, nw_ref,
                       *rest, HG, P, H):
    ya_ref, hn_ref = rest[-2:]
    sb = pl.program_id(0)
    g = pl.program_id(1)
    SB = h0_ref.shape[0]
    TP = yp_ref.shape[1]
    for i in range(SB):
        hmat = h0_ref[i]
        yoff = _dot_nt(cs_ref[i], hmat.astype(BF16))
        y = yp_ref[i] + yoff[0:TP, :] * ea_ref[i]
        gt = y * _silu(z_ref[i])
        ya_ref[i] = _rms(gt, nw_ref[...])
        s_new = _dot_tn(xdl_ref[i], bs_ref[i])
        for hh in range(HG):
            cd = cd_ref[(sb * SB + i) * H + g * HG + hh]
            rs = slice(hh * P, (hh + 1) * P)
            hn_ref[i, rs, :] = hmat[rs, :] * cd + s_new[rs, :]


def _ssd_sample_b(cd_flat, h0_all, layer, h_acc, cs_t, bs_t, xdl_t, ea_t, yp_t, z_t, nw, cfg, sb):
    G, HG, P, N = cfg["G"], cfg["HG"], cfg["P"], cfg["N"]
    GW = HG * P
    _, SQ, I, _ = h0_all.shape
    TP = yp_t.shape[1]
    tok = lambda r, w: pl.BlockSpec((sb, r, w), lambda s, g: (s, 0, g))
    state = pl.BlockSpec((None, sb, GW, N), lambda s, g: (layer, s, g, 0))
    in_specs = [
        pl.BlockSpec(memory_space=pltpu.SMEM),
        state,
        tok(BF16_ROWS, N), tok(BF16_ROWS, N), tok(BF16_ROWS, GW),
        tok(TP, GW), tok(TP, GW), tok(TP, GW),
        pl.BlockSpec((1, GW), lambda s, g: (0, g)),
    ]
    args = [cd_flat, h0_all, cs_t, bs_t, xdl_t, ea_t, yp_t, z_t, nw]
    aliases = {}
    if h_acc is not None:
        in_specs.append(pl.BlockSpec(memory_space=pl.ANY))
        args.append(h_acc)
        aliases = {len(args) - 1: 1}
    return pl.pallas_call(
        functools.partial(_ssd_sample_b_body, HG=HG, P=P, H=G * HG),
        grid=(SQ // sb, G),
        in_specs=in_specs,
        out_specs=[tok(TP, GW), state],
        out_shape=[
            jax.ShapeDtypeStruct((SQ, TP, I), F32),
            jax.ShapeDtypeStruct(h0_all.shape, F32),
        ],
        input_output_aliases=aliases,
        compiler_params=_cparams("parallel", "parallel"),
        name="ssd_sample_b",
    )(*args)


def _mix_prompt_body(u_ref, v_ref, sb_ref, sc_ref, sh_ref, lnw_ref, lnb_ref, ws_ref, be_ref, cw_ref,
                     yb_ref, yc_ref, cst_ref, qpad, vb_scr):
    c = pl.program_id(1)
    R, W = u_ref.shape
    SG = ws_ref.shape[0]
    GWS = W // SG

    @pl.when(c == 0)
    def _():
        qpad[0:SUBLANES, :] = jnp.zeros((SUBLANES, W), F32)

    row = lax.broadcasted_iota(jnp.int32, (CHUNK, CHUNK), 0)
    col = lax.broadcasted_iota(jnp.int32, (CHUNK, CHUNK), 1)
    tri = row >= col
    for j in range(R // CHUNK):
        rs = slice(j * CHUNK, (j + 1) * CHUNK)
        v = _layernorm(jax.nn.gelu(v_ref[rs, :]), lnw_ref[...], lnb_ref[...])
        vb_scr[...] = v.astype(BF16)
        for g in range(SG):
            gs = slice(g * GWS, (g + 1) * GWS)
            w = jnp.where(tri, ws_ref[g], 0.0).astype(BF16)
            mixed = _dot(w, vb_scr[:, gs]) + be_ref[:, gs]
            yb_ref[rs, gs] = (jax.nn.gelu(u_ref[rs, gs]) * mixed).astype(BF16)

    qpad[SUBLANES:SUBLANES + R, :] = sc_ref[...] * sh_ref[...]
    for g in range(SG):
        gs = slice(g * GWS, (g + 1) * GWS)
        yc_ref[:, gs] = (sb_ref[:, gs] * _causal_taps(qpad[:, gs], cw_ref[:, gs])).astype(BF16)
    qpad[0:SUBLANES, :] = qpad[R:R + SUBLANES, :]

    @pl.when(c == pl.num_programs(1) - 1)
    def _():
        cst_ref[0] = qpad[R:R + SUBLANES, :]


def _mix_prompt(proj, lw, cfg, batch, seq, r):
    W = cfg["W"]
    off = cfg["off"]
    nr = seq // r
    tok = lambda name: pl.BlockSpec((r, W), lambda b, c: (b * nr + c, off[name] // W))
    fixed = lambda shape: pl.BlockSpec(shape, lambda b, c: tuple(0 for _ in shape))
    out = pl.BlockSpec((r, W), lambda b, c: (b * nr + c, 0))
    return pl.pallas_call(
        _mix_prompt_body,
        grid=(batch, nr),
        in_specs=[
            tok("u"), tok("v"), tok("scb"), tok("scc"), tok("sch"),
            fixed((1, W)), fixed((1, W)), fixed(lw["w_spatial"].shape), fixed((CHUNK, W)),
            fixed(lw["conv_c_w"].shape),
        ],
        out_specs=[out, out, pl.BlockSpec((1, SUBLANES, W), lambda b, c: (b, 0, 0))],
        out_shape=[
            jax.ShapeDtypeStruct((batch * seq, W), BF16),
            jax.ShapeDtypeStruct((batch * seq, W), BF16),
            jax.ShapeDtypeStruct((batch, SUBLANES, W), F32),
        ],
        scratch_shapes=[pltpu.VMEM((r + SUBLANES, W), F32), pltpu.VMEM((CHUNK, W), BF16)],
        compiler_params=_cparams("parallel", "arbitrary"),
        name="mix_prompt",
    )(proj, proj, proj, proj, proj, lw["sgu_ln_w"], lw["sgu_ln_b"], lw["w_spatial"], lw["b_spatial_e"],
      lw["conv_c_w"])


def _mix_sample_stats_body(v_ref, lnw_ref, lnb_ref, vo_ref):
    for l in range(v_ref.shape[0]):
        vo_ref[l] = _layernorm(jax.nn.gelu(v_ref[l]), lnw_ref[...], lnb_ref[...])


def _mix_sample_body(u_ref, v_ref, sb_ref, sc_ref, sh_ref, pq_ref, we_ref, be_ref, cw_ref,
                     yb_ref, yc_ref, cst_ref):
    T = u_ref.shape[0]
    K = cw_ref.shape[0]
    vs = [v_ref[l] for l in range(T)]
    for l in range(T):
        mixed = be_ref[l:l + 1, :]
        for s in range(T):
            mixed = mixed + we_ref[l * T + s:l * T + s + 1, :] * vs[s]
        yb_ref[l] = (jax.nn.gelu(u_ref[l]) * mixed).astype(BF16)
    rows = [pq_ref[j] for j in range(K - 1)] + [sc_ref[l] * sh_ref[l] for l in range(T)]
    for j in range(K - 1):
        cst_ref[j] = rows[T + j]
    for l in range(T):
        conv = rows[l] * cw_ref[0:1, :]
        for j in range(1, K):
            conv = conv + rows[l + j] * cw_ref[j:j + 1, :]
        yc_ref[l] = (sb_ref[l] * conv).astype(BF16)


def _mix_sample(proj3, convc_tm, lw, cfg, wc):
    W = cfg["W"]
    off = cfg["off"]
    T, SQ, _ = proj3.shape
    K = lw["conv_c_w"].shape[0]
    v_norm = pl.pallas_call(
        _mix_sample_stats_body,
        grid=(1,),
        in_specs=[
            pl.BlockSpec((T, SQ, W), lambda i: (0, 0, off["v"] // W)),
            pl.BlockSpec((1, W), lambda i: (0, 0)),
            pl.BlockSpec((1, W), lambda i: (0, 0)),
        ],
        out_specs=pl.BlockSpec((T, SQ, W), lambda i: (0, 0, 0)),
        out_shape=jax.ShapeDtypeStruct((T, SQ, W), F32),
        compiler_params=_cparams("arbitrary"),
        name="mix_sample_norm",
    )(proj3, lw["sgu_ln_w"], lw["sgu_ln_b"])
    tok = lambda name: pl.BlockSpec((T, SQ, wc), lambda j: (0, 0, off[name] // wc + j))
    loc = lambda r: pl.BlockSpec((r, SQ, wc), lambda j: (0, 0, j))
    par = lambda r: pl.BlockSpec((r, wc), lambda j: (0, j))
    yb, yc, cst = pl.pallas_call(
        _mix_sample_body,
        grid=(W // wc,),
        in_specs=[tok("u"), loc(T), tok("scb"), tok("scc"), tok("sch"), loc(K - 1),
                  par(T * T), par(T), par(K)],
        out_specs=[loc(T), loc(T), loc(K - 1)],
        out_shape=[
            jax.ShapeDtypeStruct((T, SQ, W), BF16),
            jax.ShapeDtypeStruct((T, SQ, W), BF16),
            jax.ShapeDtypeStruct((K - 1, SQ, W), F32),
        ],
        compiler_params=_cparams("parallel"),
        name="mix_sample",
    )(proj3, v_norm, proj3, proj3, proj3, convc_tm, lw["w_sub_e"], lw["b_sub_e"], lw["conv_c_w"])
    return yb, yc, cst, v_norm


def _merge_body(ya_ref, yb_ref, yc_ref, ga_ref, gb_ref, gc_ref, wa_ref, wb_ref, wc_ref, o_ref):
    merged = (_sigmoid(ga_ref[...]) * _dot(ya_ref[...], wa_ref[...])
              + _sigmoid(gb_ref[...]) * _dot(yb_ref[...], wb_ref[...])
              + _sigmoid(gc_ref[...]) * _dot(yc_ref[...], wc_ref[...]))
    o_ref[...] = merged.astype(BF16)


def _merge(ya, yb, yc, proj, d, ws, layer, cfg, *, tm, tn):
    rows = ya.shape[0]
    I, W, SCW = ya.shape[1], yb.shape[1], yc.shape[1]
    go = cfg["off"]["gate"] // tn
    act = lambda w: pl.BlockSpec((tm, w), lambda m, j: (m, 0))
    gate = lambda k: pl.BlockSpec((tm, tn), lambda m, j: (m, go + k * (d // tn) + j))
    wcol = lambda k: pl.BlockSpec((None, k, tn), lambda m, j: (layer, 0, j))
    return pl.pallas_call(
        _merge_body,
        grid=(rows // tm, d // tn),
        in_specs=[act(I), act(W), act(SCW), gate(0), gate(1), gate(2), wcol(I), wcol(W), wcol(SCW)],
        out_specs=pl.BlockSpec((tm, tn), lambda m, j: (m, j)),
        out_shape=jax.ShapeDtypeStruct((rows, d), BF16),
        compiler_params=_cparams("parallel", "parallel"),
        name="merge",
    )(ya, yb, yc, proj, proj, proj, ws["w_out_a"], ws["w_out_b"], ws["w_out_c"])


def _ffn_body(x_ref, mg_ref, wo_ref, nw_ref, wg_ref, wu_ref, wd_ref, fw_ref, o_ref, h_scr, *, sub, final):
    f = pl.program_id(1)
    tm = x_ref.shape[0]

    @pl.when(f == 0)
    def _():
        o_ref[...] = x_ref[...] + _dot(mg_ref[...], wo_ref[...])

        def rows(i, carry):
            r = pl.ds(pl.multiple_of(i * sub, sub), sub)
            h_scr[r, :] = _rms(o_ref[r, :], nw_ref[...]).astype(BF16)
            return carry
        lax.fori_loop(0, tm // sub, rows, 0)

    h = h_scr[...]
    act = _silu(_dot(h, wg_ref[...])) * _dot(h, wu_ref[...])
    o_ref[...] += _dot(act.astype(BF16), wd_ref[...])

    if final:
        @pl.when(f == pl.num_programs(1) - 1)
        def _():
            def rows(i, carry):
                r = pl.ds(pl.multiple_of(i * sub, sub), sub)
                o_ref[r, :] = _rms(o_ref[r, :], fw_ref[...])
                return carry
            lax.fori_loop(0, tm // sub, rows, 0)


def _ffn(x, merged, nw, ws, layer, final_w, *, tm, tf, final):
    rows, d = x.shape
    fh = ws["w_gate"].shape[2]
    act = pl.BlockSpec((tm, d), lambda m, f: (m, 0))
    vec = pl.BlockSpec((1, d), lambda m, f: (0, 0))
    wcol = pl.BlockSpec((None, d, tf), lambda m, f: (layer, 0, f))
    wo = pl.BlockSpec((None, d, d), lambda m, f: (layer, 0, 0), pipeline_mode=pl.Buffered(1))
    return pl.pallas_call(
        functools.partial(_ffn_body, sub=min(tm, 128), final=final),
        grid=(rows // tm, fh // tf),
        in_specs=[act, act, wo, vec, wcol, wcol, pl.BlockSpec((None, tf, d), lambda m, f: (layer, f, 0)), vec],
        out_specs=act,
        out_shape=jax.ShapeDtypeStruct((rows, d), F32),
        scratch_shapes=[pltpu.VMEM((tm, d), BF16)],
        compiler_params=_cparams("parallel", "arbitrary"),
        name="ffn",
    )(x, merged, ws["w_o"], nw, ws["w_gate"], ws["w_up"], ws["w_down"], final_w)


def _tile(n, pref, quantum=SUBLANES):
    for t in range(min(n, pref), 0, -1):
        if n % t == 0 and t % quantum == 0:
            return t
    raise ValueError((n, pref, quantum))


def _pad_rows(a, rows):
    a = jnp.swapaxes(a, 0, 1)
    return jnp.pad(a, ((0, 0), (0, rows - a.shape[1]), (0, 0)))


def kernel(x_prompt, x_sample, state_conv_a, state_ssm, state_conv_c, norm_mix_w, w_in, conv_a_w, conv_a_b,
           dt_bias, a_log, d_skip, ssm_norm_w, w_out_a, sgu_ln_w, sgu_ln_b, w_spatial, b_spatial, w_out_b,
           conv_c_w, w_out_c, w_o, norm_ffn_w, w_gate, w_up, w_down, norm_final_w):
    depth, d, _ = w_in.shape
    batch, seq, _ = x_prompt.shape
    sq, t_dec, _ = x_sample.shape
    I = ssm_norm_w.shape[1]
    H = dt_bias.shape[1]
    P, N = state_ssm.shape[3], state_ssm.shape[4]
    CD = conv_a_w.shape[2]
    G = (CD - I) // (2 * N)
    HG = H // G
    W = sgu_ln_w.shape[1]
    SG = w_spatial.shape[1]
    SCW = conv_c_w.shape[2]
    assert W == SCW and H <= LANES and LANES % P == 0 and (HG * P) % LANES == 0
    assert seq % CHUNK == 0 and t_dec <= SUBLANES and w_spatial.shape[2] == CHUNK

    sizes = (("z", I), ("x", I), ("bc", CD - I), ("u", W), ("v", W), ("scb", SCW), ("scc", SCW), ("sch", SCW),
             ("gate", 3 * d))
    off, o = {}, 0
    for name, s in sizes:
        off[name] = o
        o += s
    n_main = o
    cfg = dict(G=G, HG=HG, P=P, N=N, W=W, off=off)
    dt0 = I + CD

    w_in_t = jnp.swapaxes(w_in, 1, 2)
    w_in_main = _prep_w_in(w_in_t, dt0, H, n_main, _tile(math.gcd(dt0, n_main), 512, LANES))
    wdt_hi, wdt_lo = _prep_w_dt(w_in_t, dt0, H)
    ws = dict(w_out_a=w_out_a.astype(BF16), w_out_b=w_out_b.astype(BF16), w_out_c=w_out_c.astype(BF16),
              w_o=w_o.astype(BF16), w_gate=w_gate.astype(BF16), w_up=w_up.astype(BF16),
              w_down=w_down.astype(BF16))

    head_of_chan = jnp.arange(I) // P
    expand = (jnp.arange(LANES)[:, None] == head_of_chan[None, :]).astype(BF16)
    pos = (PAST_LEN + jnp.arange(t_dec)) % CHUNK
    padh = lambda v: jnp.pad(v, (0, LANES - H)).reshape(1, LANES)

    layers = []
    for i in range(depth):
        w_sub = jnp.tril(w_spatial[i])[:, pos[:, None], pos[None, :]]
        layers.append(dict(
            norm_mix_w=norm_mix_w[i].reshape(1, d), wdh=wdt_hi[i], wdl=wdt_lo[i],
            conv_a_w=conv_a_w[i], conv_a_b=conv_a_b[i].reshape(1, CD),
            dt_bias=padh(dt_bias[i]), a_log=padh(a_log[i]),
            d_skip_e=jnp.repeat(d_skip[i], P).reshape(1, I), ssm_norm_w=ssm_norm_w[i].reshape(1, I),
            expand=expand,
            sgu_ln_w=sgu_ln_w[i].reshape(1, W), sgu_ln_b=sgu_ln_b[i].reshape(1, W),
            w_spatial=w_spatial[i],
            b_spatial_e=jnp.repeat(b_spatial[i].T, W // SG, axis=1),
            w_sub_e=jnp.repeat(jnp.transpose(w_sub, (1, 2, 0)).reshape(t_dec * t_dec, SG), W // SG, axis=1),
            b_sub_e=jnp.repeat(b_spatial[i][:, pos].T, W // SG, axis=1),
            conv_c_w=conv_c_w[i],
            norm_ffn_w=norm_ffn_w[i].reshape(1, d),
        ))
    final_w = norm_final_w.reshape(1, d)

    rows_p = batch * seq
    rows_s = sq * t_dec
    yp = x_prompt.reshape(rows_p, d)
    ys = jnp.swapaxes(x_sample, 0, 1).reshape(rows_s, d)
    tn_in = _tile(n_main, 1024, LANES)
    fh = w_gate.shape[2]
    tf = _tile(fh, 512, LANES)
    state_all = state_ssm.reshape(depth, sq, I, N)

    ca_p, ssm_p, cc_p, ca_s, cc_s, v_s = [], [], [], [], [], []
    h_acc = None
    for i, lw in enumerate(layers):
        last = i == depth - 1
        proj, dt_raw = _inproj(yp, lw["norm_mix_w"], w_in_main, i, lw["wdh"], lw["wdl"],
                               tm=_tile(rows_p, 1024), tn=tn_in)
        ya, cst_a, h_new = _ssd_prompt(proj, dt_raw, lw, cfg, batch, seq)
        yb, yc, cst_c = _mix_prompt(proj, lw, cfg, batch, seq, _tile(seq, 256, CHUNK))
        mg = _merge(ya, yb, yc, proj, d, ws, i, cfg, tm=_tile(rows_p, 1024), tn=_tile(d, 256, LANES))
        yp = _ffn(yp, mg, lw["norm_ffn_w"], ws, i, final_w, tm=_tile(rows_p, 512), tf=tf, final=last)
        K = conv_a_w.shape[1]
        ca_p.append(cst_a[:, SUBLANES - (K - 1):, :])
        ssm_p.append(h_new.reshape(batch, H, P, N))
        cc_p.append(cst_c[:, SUBLANES - (conv_c_w.shape[1] - 1):, :])

        proj, dt_raw = _inproj(ys, lw["norm_mix_w"], w_in_main, i, lw["wdh"], lw["wdl"],
                               tm=_tile(rows_s, 512), tn=tn_in)
        proj3 = proj.reshape(t_dec, sq, n_main)
        dt3 = dt_raw.reshape(t_dec, sq, LANES)
        ypart, xdl, eae, bs, cs, cd, nx, nb, nc = _ssd_sample_a(
            proj3, dt3, jnp.swapaxes(state_conv_a[i], 0, 1), lw, cfg)
        ya_t, h_acc = _ssd_sample_b(
            cd[:, :H].reshape(sq * H), state_all, i, h_acc,
            _pad_rows(cs, BF16_ROWS), _pad_rows(bs, BF16_ROWS), _pad_rows(xdl.astype(BF16), BF16_ROWS),
            _pad_rows(eae, SUBLANES), _pad_rows(ypart, SUBLANES), _pad_rows(proj3[:, :, :I], SUBLANES),
            lw["ssm_norm_w"], cfg, _tile(sq, 32))
        ya = jnp.swapaxes(ya_t[:, :t_dec, :], 0, 1).reshape(rows_s, I).astype(BF16)
        yb, yc, cst_c, v_norm = _mix_sample(proj3, jnp.swapaxes(state_conv_c[i], 0, 1), lw, cfg,
                                            _tile(W, 512, LANES))
        mg = _merge(ya, yb.reshape(rows_s, W), yc.reshape(rows_s, SCW), proj, d, ws, i, cfg,
                    tm=_tile(rows_s, 1024), tn=_tile(d, 256, LANES))
        ys = _ffn(ys, mg, lw["norm_ffn_w"], ws, i, final_w, tm=_tile(rows_s, 512), tf=tf, final=last)
        ca_s.append(jnp.swapaxes(jnp.concatenate([nx, nb, nc], axis=-1), 0, 1))
        cc_s.append(jnp.swapaxes(cst_c, 0, 1))
        v_s.append(jnp.swapaxes(v_norm, 0, 1))

    y_prompt = yp.reshape(batch, seq, d)
    y_sample = jnp.swapaxes(ys.reshape(t_dec, sq, d), 0, 1)
    return (y_prompt, y_sample, jnp.stack(ca_p), jnp.stack(ssm_p), jnp.stack(cc_p),
            jnp.stack(ca_s), h_acc.reshape(depth, sq, H, P, N), jnp.stack(cc_s), jnp.stack(v_s))
```

```python
import functools
import math

import jax
import jax.numpy as jnp
from jax import lax
from jax.experimental import pallas as pl
from jax.experimental.pallas import tpu as pltpu

F32 = jnp.float32
BF16 = jnp.bfloat16

RMS_EPS = 1e-6
LN_EPS = 1e-5
PAST_LEN = 16384
CHUNK = 128
LANES = 128
SUBLANES = 8
BF16_ROWS = 16
NEG_BIG = -1e30
VMEM_LIMIT_BYTES = 56 * 1024 * 1024


def _cparams(*sem):
    return pltpu.CompilerParams(dimension_semantics=sem, vmem_limit_bytes=VMEM_LIMIT_BYTES)


def _dot(a, b):
    return jnp.dot(a, b, preferred_element_type=F32)


def _dot_nt(a, b):
    return lax.dot_general(a, b, (((1,), (1,)), ((), ())), preferred_element_type=F32)


def _dot_tn(a, b):
    return lax.dot_general(a, b, (((0,), (0,)), ((), ())), preferred_element_type=F32)


def _split3(x):
    hi = x.astype(BF16)
    r = x - hi.astype(F32)
    mid = r.astype(BF16)
    lo = (r - mid.astype(F32)).astype(BF16)
    return hi, mid, lo


def _dot_exact_rhs(x, m_bf16):
    hi, mid, lo = _split3(x)
    return _dot(hi, m_bf16) + _dot(mid, m_bf16) + _dot(lo, m_bf16)


def _dot_exact_lhs(m_bf16, x):
    hi, mid, lo = _split3(x)
    return _dot(m_bf16, hi) + _dot(m_bf16, mid) + _dot(m_bf16, lo)


def _sigmoid(x):
    return 0.5 * jnp.tanh(0.5 * x) + 0.5


def _silu(x):
    return x * _sigmoid(x)


def _softplus(x):
    return jnp.maximum(x, 0.0) + jnp.log1p(jnp.exp(-jnp.abs(x)))


def _rms(x, w):
    return x * lax.rsqrt(jnp.mean(x * x, axis=-1, keepdims=True) + RMS_EPS) * w


def _causal_taps(v, w):
    k = w.shape[0]
    acc = v * w[0:1, :]
    for j in range(1, k):
        acc = pltpu.roll(acc, 1, axis=0) + v * w[j:j + 1, :]
    return acc[SUBLANES:, :]


def _layernorm(x, w, b):
    mu = jnp.mean(x, axis=-1, keepdims=True)
    d = x - mu
    var = jnp.mean(d * d, axis=-1, keepdims=True)
    return d * lax.rsqrt(var + LN_EPS) * w + b


def _inproj_body(x_ref, nw_ref, w_ref, wdh_ref, wdl_ref, o_ref, dt_ref, h_scr, *, sub):
    tm = x_ref.shape[0]

    @pl.when(pl.program_id(1) == 0)
    def _():
        def rows(i, carry):
            r = pl.ds(pl.multiple_of(i * sub, sub), sub)
            h = _rms(x_ref[r, :], nw_ref[...])
            hb = h.astype(BF16)
            h_scr[r, :] = hb
            hl = (h - hb.astype(F32)).astype(BF16)
            dt_ref[r, :] = _dot(hb, wdh_ref[...]) + _dot(hb, wdl_ref[...]) + _dot(hl, wdh_ref[...])
            return carry
        lax.fori_loop(0, tm // sub, rows, 0)

    o_ref[...] = _dot(h_scr[...], w_ref[...])


def _prep_w_in_body(a_ref, b_ref, o_ref, *, j0, shift, sub):
    tw, d = a_ref.shape
    j = pl.program_id(1)

    @pl.when(j < j0)
    def _():
        for k in range(d // sub):
            ks = slice(k * sub, (k + 1) * sub)
            o_ref[ks, :] = a_ref[:, ks].T.astype(BF16)

    @pl.when(j >= j0)
    def _():
        for k in range(d // sub):
            ks = slice(k * sub, (k + 1) * sub)
            src = jnp.concatenate([a_ref[shift:, ks], b_ref[:, ks]], axis=0)
            o_ref[ks, :] = src.T.astype(BF16)


def _prep_w_in(w_in_t, dt0, shift, n_main, tw):
    depth, _, d = w_in_t.shape
    assert dt0 % tw == 0 and n_main % tw == 0 and tw % shift == 0 and shift % SUBLANES == 0
    j0 = dt0 // tw
    per = tw // shift
    return pl.pallas_call(
        functools.partial(_prep_w_in_body, j0=j0, shift=shift, sub=min(d, 256)),
        grid=(depth, n_main // tw),
        in_specs=[
            pl.BlockSpec((None, tw, d), lambda l, j: (l, j, 0)),
            pl.BlockSpec((None, shift, d), lambda l, j: (l, jnp.where(j < j0, 0, per * (j + 1)), 0)),
        ],
        out_specs=pl.BlockSpec((None, d, tw), lambda l, j: (l, 0, j)),
        out_shape=jax.ShapeDtypeStruct((depth, d, n_main), BF16),
        compiler_params=_cparams("parallel", "parallel"),
        name="prep_w_in",
    )(w_in_t, w_in_t)


def _prep_w_dt_body(a_ref, hi_ref, lo_ref, *, sub):
    h, d = a_ref.shape
    for k in range(d // sub):
        ks = slice(k * sub, (k + 1) * sub)
        src = jnp.concatenate([a_ref[:, ks], jnp.zeros((LANES - h, sub), F32)], axis=0)
        w = src.T
        hi = w.astype(BF16)
        hi_ref[ks, :] = hi
        lo_ref[ks, :] = (w - hi.astype(F32)).astype(BF16)


def _prep_w_dt(w_in_t, dt0, h):
    depth, _, d = w_in_t.shape
    assert dt0 % h == 0 and h % SUBLANES == 0 and h <= LANES
    out = pl.BlockSpec((None, d, LANES), lambda l: (l, 0, 0))
    return pl.pallas_call(
        functools.partial(_prep_w_dt_body, sub=min(d, 256)),
        grid=(depth,),
        in_specs=[pl.BlockSpec((None, h, d), lambda l: (l, dt0 // h, 0))],
        out_specs=[out, out],
        out_shape=[jax.ShapeDtypeStruct((depth, d, LANES), BF16)] * 2,
        compiler_params=_cparams("parallel"),
        name="prep_w_dt",
    )(w_in_t)


def _inproj(x, nw, w, layer, wdh, wdl, *, tm, tn):
    rows, d = x.shape
    n = w.shape[2]
    return pl.pallas_call(
        functools.partial(_inproj_body, sub=min(tm, 128)),
        grid=(rows // tm, n // tn),
        in_specs=[
            pl.BlockSpec((tm, d), lambda m, j: (m, 0)),
            pl.BlockSpec((1, d), lambda m, j: (0, 0)),
            pl.BlockSpec((None, d, tn), lambda m, j: (layer, 0, j)),
            pl.BlockSpec((d, LANES), lambda m, j: (0, 0)),
            pl.BlockSpec((d, LANES), lambda m, j: (0, 0)),
        ],
        out_specs=[
            pl.BlockSpec((tm, tn), lambda m, j: (m, j)),
            pl.BlockSpec((tm, LANES), lambda m, j: (m, 0)),
        ],
        out_shape=[jax.ShapeDtypeStruct((rows, n), F32), jax.ShapeDtypeStruct((rows, LANES), F32)],
        scratch_shapes=[pltpu.VMEM((tm, d), BF16)],
        compiler_params=_cparams("parallel", "arbitrary"),
        name="inproj",
    )(x, nw, w, wdh, wdl)


def _mixers_prompt_body(z_ref, x_ref, bc_ref, dt_ref, cw_ref, cb_ref, dtb_ref, alog_ref, dsk_ref, nw_ref,
                        e_ref, u_ref, v_ref, sb_ref, sc_ref, sh_ref, lnw_ref, lnb_ref, ws_ref, be_ref, ccw_ref,
                        ya_ref, cst_ref, h_ref, yb_ref, yc_ref, cstc_ref, xpad, bcpad, qpad, vb_scr,
                        *, G, HG, P, N):
    c = pl.program_id(1)
    L = CHUNK
    GW = HG * P
    I = G * GW
    W = u_ref.shape[1]
    SG = ws_ref.shape[0]
    GWS = W // SG

    @pl.when(c == 0)
    def _():
        h_ref[...] = jnp.zeros(h_ref.shape, F32)
        xpad[0:SUBLANES, :] = jnp.zeros((SUBLANES, I), F32)
        bcpad[0:SUBLANES, :] = jnp.zeros((SUBLANES, 2 * G * N), F32)
        qpad[0:SUBLANES, :] = jnp.zeros((SUBLANES, W), F32)

    xpad[SUBLANES:SUBLANES + L, :] = x_ref[...]
    bcpad[SUBLANES:SUBLANES + L, :] = bc_ref[...]
    qpad[SUBLANES:SUBLANES + L, :] = sc_ref[...] * sh_ref[...]
    vb_scr[...] = _layernorm(jax.nn.gelu(v_ref[...]), lnw_ref[...], lnb_ref[...]).astype(BF16)

    def gmlp_and_short_conv(g):
        gs = slice(g * GWS, (g + 1) * GWS)
        w = jnp.where(tri, ws_ref[g], 0.0).astype(BF16)
        mixed = _dot(w, vb_scr[:, gs]) + be_ref[:, gs]
        yb_ref[:, gs] = (jax.nn.gelu(u_ref[:, gs]) * mixed).astype(BF16)
        yc_ref[:, gs] = (sb_ref[:, gs] * _causal_taps(qpad[:, gs], ccw_ref[:, gs])).astype(BF16)

    def conv(pad, lo, hi, col0):
        acc = _causal_taps(pad[:, lo:hi], cw_ref[:, col0 + lo:col0 + hi])
        return _silu(acc + cb_ref[:, col0 + lo:col0 + hi])

    dt = _softplus(dt_ref[...] + dtb_ref[...])
    a = -jnp.exp(alog_ref[...])
    row = lax.broadcasted_iota(jnp.int32, (L, L), 0)
    col = lax.broadcasted_iota(jnp.int32, (L, L), 1)
    tri = row >= col
    tril_ones = jnp.where(tri, 1.0, 0.0).astype(BF16)
    acum = _dot_exact_lhs(tril_ones, dt * a)
    acum_t = acum.T
    last = acum[L - 1:L, :]
    dl = jnp.exp(last - acum)
    ea = jnp.exp(acum)
    cd_t = jnp.exp(acum_t[:, L - 1:L])
    dt_s, dl_s, ea_s = _split3(dt), _split3(dl), _split3(ea)
    lane = lax.broadcasted_iota(jnp.int32, (L, LANES), 1)
    HPL = LANES // P

    for g in range(G):
        gs = slice(g * GW, (g + 1) * GW)
        e_g = e_ref[:, gs]

        def expand(parts):
            return _dot(parts[0], e_g) + _dot(parts[1], e_g) + _dot(parts[2], e_g)

        xg = conv(xpad, g * GW, (g + 1) * GW, 0)
        xdt = xg * expand(dt_s)
        bb = conv(bcpad, g * N, (g + 1) * N, I).astype(BF16)
        cb16 = conv(bcpad, (G + g) * N, (G + g + 1) * N, I).astype(BF16)
        cbm = _dot_nt(cb16, bb)
        hg16 = h_ref[0, gs, :].astype(BF16)
        y = _dot_nt(cb16, hg16) * expand(ea_s) + dsk_ref[:, gs] * xg

        tiles = []
        for j in range(GW // LANES):
            xt = xdt[:, j * LANES:(j + 1) * LANES]
            acc = jnp.zeros((L, LANES), F32)
            for hh in range(HPL):
                h = g * HG + j * HPL + hh
                seg = acum[:, h:h + 1] - acum_t[h:h + 1, :]
                dec = jnp.exp(jnp.where(tri, seg, NEG_BIG))
                m = (cbm * dec).astype(BF16)
                inhead = (lane >= hh * P) & (lane < (hh + 1) * P)
                acc = acc + _dot(m, jnp.where(inhead, xt, 0.0).astype(BF16))
            tiles.append(acc)
        y = y + jnp.concatenate(tiles, axis=1)

        gt = y * _silu(z_ref[:, gs])
        ya_ref[:, gs] = _rms(gt, nw_ref[:, gs]).astype(BF16)

        s_new = _dot_tn((xdt * expand(dl_s)).astype(BF16), bb)
        for hh in range(HG):
            h = g * HG + hh
            rs = slice(g * GW + hh * P, g * GW + (hh + 1) * P)
            h_ref[0, rs, :] = h_ref[0, rs, :] * cd_t[h:h + 1, :] + s_new[hh * P:(hh + 1) * P, :]

        if g < SG:
            gmlp_and_short_conv(g)
    for g in range(G, SG):
        gmlp_and_short_conv(g)

    xpad[0:SUBLANES, :] = xpad[L:L + SUBLANES, :]
    bcpad[0:SUBLANES, :] = bcpad[L:L + SUBLANES, :]
    qpad[0:SUBLANES, :] = qpad[L:L + SUBLANES, :]

    @pl.when(c == pl.num_programs(1) - 1)
    def _():
        cst_ref[0, :, 0:I] = xpad[L:L + SUBLANES, :]
        cst_ref[0, :, I:] = bcpad[L:L + SUBLANES, :]
        cstc_ref[0] = qpad[L:L + SUBLANES, :]


def _mixers_prompt(proj, dt_raw, lw, cfg, batch, seq):
    G, HG, P, N, W = cfg["G"], cfg["HG"], cfg["P"], cfg["N"], cfg["W"]
    I = G * HG * P
    BC = 2 * G * N
    CD = I + BC
    nc = seq // CHUNK
    off = cfg["off"]
    row = lambda b, c: b * nc + c
    full = lambda b, c: (0, 0)
    tok = lambda name: pl.BlockSpec((CHUNK, W), lambda b, c: (row(b, c), off[name] // W))
    return pl.pallas_call(
        functools.partial(_mixers_prompt_body, G=G, HG=HG, P=P, N=N),
        grid=(batch, nc),
        in_specs=[
            pl.BlockSpec((CHUNK, I), lambda b, c: (row(b, c), off["z"] // I)),
            pl.BlockSpec((CHUNK, I), lambda b, c: (row(b, c), off["x"] // I)),
            pl.BlockSpec((CHUNK, BC), lambda b, c: (row(b, c), off["bc"] // BC)),
            pl.BlockSpec((CHUNK, LANES), lambda b, c: (row(b, c), 0)),
            pl.BlockSpec(lw["conv_a_w"].shape, full),
            pl.BlockSpec((1, CD), full),
            pl.BlockSpec((1, LANES), full),
            pl.BlockSpec((1, LANES), full),
            pl.BlockSpec((1, I), full),
            pl.BlockSpec((1, I), full),
            pl.BlockSpec((LANES, I), full),
            tok("u"), tok("v"), tok("scb"), tok("scc"), tok("sch"),
            pl.BlockSpec((1, W), full), pl.BlockSpec((1, W), full),
            pl.BlockSpec(lw["w_spatial"].shape, lambda b, c: (0, 0, 0)),
            pl.BlockSpec((CHUNK, W), full),
            pl.BlockSpec(lw["conv_c_w"].shape, full),
        ],
        out_specs=[
            pl.BlockSpec((CHUNK, I), lambda b, c: (row(b, c), 0)),
            pl.BlockSpec((1, SUBLANES, CD), lambda b, c: (b, 0, 0)),
            pl.BlockSpec((1, I, N), lambda b, c: (b, 0, 0)),
            pl.BlockSpec((CHUNK, W), lambda b, c: (row(b, c), 0)),
            pl.BlockSpec((CHUNK, W), lambda b, c: (row(b, c), 0)),
            pl.BlockSpec((1, SUBLANES, W), lambda b, c: (b, 0, 0)),
        ],
        out_shape=[
            jax.ShapeDtypeStruct((batch * seq, I), BF16),
            jax.ShapeDtypeStruct((batch, SUBLANES, CD), F32),
            jax.ShapeDtypeStruct((batch, I, N), F32),
            jax.ShapeDtypeStruct((batch * seq, W), BF16),
            jax.ShapeDtypeStruct((batch * seq, W), BF16),
            jax.ShapeDtypeStruct((batch, SUBLANES, W), F32),
        ],
        scratch_shapes=[
            pltpu.VMEM((CHUNK + SUBLANES, I), F32),
            pltpu.VMEM((CHUNK + SUBLANES, BC), F32),
            pltpu.VMEM((CHUNK + SUBLANES, W), F32),
            pltpu.VMEM((CHUNK, W), BF16),
        ],
        compiler_params=_cparams("parallel", "arbitrary"),
        name="mixers_prompt",
    )(proj, proj, proj, dt_raw, lw["conv_a_w"], lw["conv_a_b"], lw["dt_bias"], lw["a_log"],
      lw["d_skip_e"], lw["ssm_norm_w"], lw["expand"],
      proj, proj, proj, proj, proj, lw["sgu_ln_w"], lw["sgu_ln_b"], lw["w_spatial"], lw["b_spatial_e"],
      lw["conv_c_w"])


def _ssd_sample_a_body(x_ref, b_ref, c_ref, dt_ref, px_ref, pb_ref, pc_ref, cwx_ref, cwb_ref, cwc_ref,
                       cbx_ref, cbb_ref, cbc_ref, dtb_ref, alog_ref, dsk_ref, e_ref,
                       yp_ref, xdl_ref, ea_ref, bs_ref, cs_ref, cd_ref, nx_ref, nb_ref, nc_ref):
    T = x_ref.shape[0]
    K = cwx_ref.shape[0]

    def conv(p_ref, cur_ref, w_ref, bias_ref, new_ref):
        rows = [p_ref[j] for j in range(K - 1)] + [cur_ref[l] for l in range(T)]
        for j in range(K - 1):
            new_ref[j] = rows[T + j]
        outs = []
        for l in range(T):
            acc = rows[l] * w_ref[0:1, :]
            for j in range(1, K):
                acc = acc + rows[l + j] * w_ref[j:j + 1, :]
            outs.append(_silu(acc + bias_ref[...]))
        return outs

    xs = conv(px_ref, x_ref, cwx_ref, cbx_ref, nx_ref)
    bs = conv(pb_ref, b_ref, cwb_ref, cbb_ref, nb_ref)
    cs = conv(pc_ref, c_ref, cwc_ref, cbc_ref, nc_ref)

    a = -jnp.exp(alog_ref[...])
    dts, acums = [], []
    run = None
    for l in range(T):
        dt = _softplus(dt_ref[l] + dtb_ref[...])
        run = dt * a if run is None else run + dt * a
        dts.append(dt)
        acums.append(run)
    last = acums[T - 1]
    cd_ref[...] = jnp.exp(last)

    e_g = e_ref[...]

    def expand(coef):
        return _dot_exact_rhs(coef, e_g)

    xdts = [xs[l] * expand(dts[l]) for l in range(T)]
    for l in range(T):
        bs_ref[l] = bs[l].astype(BF16)
        cs_ref[l] = cs[l].astype(BF16)
        ea_ref[l] = expand(jnp.exp(acums[l]))
        xdl_ref[l] = xdts[l] * expand(jnp.exp(last - acums[l]))
        y = dsk_ref[...] * xs[l]
        for s in range(l + 1):
            cb = jnp.sum(cs[l] * bs[s], axis=-1, keepdims=True)
            y = y + expand(cb * jnp.exp(acums[l] - acums[s])) * xdts[s]
        yp_ref[l] = y


def _ssd_sample_a(proj3, dt3, conv_tm, lw, cfg):
    G, HG, P, N = cfg["G"], cfg["HG"], cfg["P"], cfg["N"]
    GW = HG * P
    I = G * GW
    T, SQ, _ = proj3.shape
    K = lw["conv_a_w"].shape[0]
    off = cfg["off"]
    xo, bo, co = off["x"] // GW, off["bc"] // N, off["bc"] // N + G
    sxo, sbo, sco = 0, I // N, I // N + G
    tok = lambda w, o: pl.BlockSpec((T, SQ, w), lambda g: (0, 0, o + g))
    pre = lambda w, o: pl.BlockSpec((K - 1, SQ, w), lambda g: (0, 0, o + g))
    par = lambda r, w, o: pl.BlockSpec((r, w), lambda g: (0, o + g))
    fixed = lambda shape: pl.BlockSpec(shape, lambda g: tuple(0 for _ in shape))
    return pl.pallas_call(
        _ssd_sample_a_body,
        grid=(G,),
        in_specs=[
            tok(GW, xo), tok(N, bo), tok(N, co), fixed((T, SQ, LANES)),
            pre(GW, sxo), pre(N, sbo), pre(N, sco),
            par(K, GW, sxo), par(K, N, sbo), par(K, N, sco),
            par(1, GW, sxo), par(1, N, sbo), par(1, N, sco),
            fixed((1, LANES)), fixed((1, LANES)), par(1, GW, 0), par(LANES, GW, 0),
        ],
        out_specs=[
            tok(GW, 0), tok(GW, 0), tok(GW, 0), tok(N, 0), tok(N, 0), fixed((SQ, LANES)),
            pre(GW, 0), pre(N, 0), pre(N, 0),
        ],
        out_shape=[
            jax.ShapeDtypeStruct((T, SQ, I), F32),
            jax.ShapeDtypeStruct((T, SQ, I), F32),
            jax.ShapeDtypeStruct((T, SQ, I), F32),
            jax.ShapeDtypeStruct((T, SQ, G * N), BF16),
            jax.ShapeDtypeStruct((T, SQ, G * N), BF16),
            jax.ShapeDtypeStruct((SQ, LANES), F32),
            jax.ShapeDtypeStruct((K - 1, SQ, I), F32),
            jax.ShapeDtypeStruct((K - 1, SQ, G * N), F32),
            jax.ShapeDtypeStruct((K - 1, SQ, G * N), F32),
        ],
        compiler_params=_cparams("arbitrary"),
        name="ssd_sample_a",
    )(proj3, proj3, proj3, dt3, conv_tm, conv_tm, conv_tm,
      lw["conv_a_w"], lw["conv_a_w"], lw["conv_a_w"], lw["conv_a_b"], lw["conv_a_b"], lw["conv_a_b"],
      lw["dt_bias"], lw["a_log"], lw["d_skip_e"], lw["expand"])


def _ssd_sample_b_body(cd_ref, h0_ref, cs_ref, bs_ref, xdl_ref, ea_ref, yp_ref, z_ref, nw_ref,
                       *rest, HG, P, H):
    ya_ref, hn_ref = rest[-2:]
    sb = pl.program_id(0)
    g = pl.program_id(1)
    SB = h0_ref.shape[0]
    TP = yp_ref.shape[1]
    for i in range(SB):
        hmat = h0_ref[i]
        yoff = _dot_nt(cs_ref[i], hmat.astype(BF16))
        y = yp_ref[i] + yoff[0:TP, :] * ea_ref[i]
        gt = y * _silu(z_ref[i])
        ya_ref[i] = _rms(gt, nw_ref[...])
        s_new = _dot_tn(xdl_ref[i], bs_ref[i])
        for hh in range(HG):
            cd = cd_ref[(sb * SB + i) * H + g * HG + hh]
            rs = slice(hh * P, (hh + 1) * P)
            hn_ref[i, rs, :] = hmat[rs, :] * cd + s_new[rs, :]


def _ssd_sample_b(cd_flat, h0_all, layer, h_acc, cs_t, bs_t, xdl_t, ea_t, yp_t, z_t, nw, cfg, sb):
    G, HG, P, N = cfg["G"], cfg["HG"], cfg["P"], cfg["N"]
    GW = HG * P
    _, SQ, I, _ = h0_all.shape
    TP = yp_t.shape[1]
    tok = lambda r, w: pl.BlockSpec((sb, r, w), lambda s, g: (s, 0, g))
    state = pl.BlockSpec((None, sb, GW, N), lambda s, g: (layer, s, g, 0))
    in_specs = [
        pl.BlockSpec(memory_space=pltpu.SMEM),
        state,
        tok(BF16_ROWS, N), tok(BF16_ROWS, N), tok(BF16_ROWS, GW),
        tok(TP, GW), tok(TP, GW), tok(TP, GW),
        pl.BlockSpec((1, GW), lambda s, g: (0, g)),
    ]
    args = [cd_flat, h0_all, cs_t, bs_t, xdl_t, ea_t, yp_t, z_t, nw]
    aliases = {}
    if h_acc is not None:
        in_specs.append(pl.BlockSpec(memory_space=pl.ANY))
        args.append(h_acc)
        aliases = {len(args) - 1: 1}
    return pl.pallas_call(
        functools.partial(_ssd_sample_b_body, HG=HG, P=P, H=G * HG),
        grid=(SQ // sb, G),
        in_specs=in_specs,
        out_specs=[tok(TP, GW), state],
        out_shape=[
            jax.ShapeDtypeStruct((SQ, TP, I), F32),
            jax.ShapeDtypeStruct(h0_all.shape, F32),
        ],
        input_output_aliases=aliases,
        compiler_params=_cparams("parallel", "parallel"),
        name="ssd_sample_b",
    )(*args)


def _mix_sample_stats_body(v_ref, lnw_ref, lnb_ref, vo_ref):
    for l in range(v_ref.shape[0]):
        vo_ref[l] = _layernorm(jax.nn.gelu(v_ref[l]), lnw_ref[...], lnb_ref[...])


def _mix_sample_body(u_ref, v_ref, sb_ref, sc_ref, sh_ref, pq_ref, we_ref, be_ref, cw_ref,
                     yb_ref, yc_ref, cst_ref):
    T = u_ref.shape[0]
    K = cw_ref.shape[0]
    vs = [v_ref[l] for l in range(T)]
    for l in range(T):
        mixed = be_ref[l:l + 1, :]
        for s in range(T):
            mixed = mixed + we_ref[l * T + s:l * T + s + 1, :] * vs[s]
        yb_ref[l] = (jax.nn.gelu(u_ref[l]) * mixed).astype(BF16)
    rows = [pq_ref[j] for j in range(K - 1)] + [sc_ref[l] * sh_ref[l] for l in range(T)]
    for j in range(K - 1):
        cst_ref[j] = rows[T + j]
    for l in range(T):
        conv = rows[l] * cw_ref[0:1, :]
        for j in range(1, K):
            conv = conv + rows[l + j] * cw_ref[j:j + 1, :]
        yc_ref[l] = (sb_ref[l] * conv).astype(BF16)


def _mix_sample(proj3, convc_tm, lw, cfg, wc):
    W = cfg["W"]
    off = cfg["off"]
    T, SQ, _ = proj3.shape
    K = lw["conv_c_w"].shape[0]
    v_norm = pl.pallas_call(
        _mix_sample_stats_body,
        grid=(1,),
        in_specs=[
            pl.BlockSpec((T, SQ, W), lambda i: (0, 0, off["v"] // W)),
            pl.BlockSpec((1, W), lambda i: (0, 0)),
            pl.BlockSpec((1, W), lambda i: (0, 0)),
        ],
        out_specs=pl.BlockSpec((T, SQ, W), lambda i: (0, 0, 0)),
        out_shape=jax.ShapeDtypeStruct((T, SQ, W), F32),
        compiler_params=_cparams("arbitrary"),
        name="mix_sample_norm",
    )(proj3, lw["sgu_ln_w"], lw["sgu_ln_b"])
    tok = lambda name: pl.BlockSpec((T, SQ, wc), lambda j: (0, 0, off[name] // wc + j))
    loc = lambda r: pl.BlockSpec((r, SQ, wc), lambda j: (0, 0, j))
    par = lambda r: pl.BlockSpec((r, wc), lambda j: (0, j))
    yb, yc, cst = pl.pallas_call(
        _mix_sample_body,
        grid=(W // wc,),
        in_specs=[tok("u"), loc(T), tok("scb"), tok("scc"), tok("sch"), loc(K - 1),
                  par(T * T), par(T), par(K)],
        out_specs=[loc(T), loc(T), loc(K - 1)],
        out_shape=[
            jax.ShapeDtypeStruct((T, SQ, W), BF16),
            jax.ShapeDtypeStruct((T, SQ, W), BF16),
            jax.ShapeDtypeStruct((K - 1, SQ, W), F32),
        ],
        compiler_params=_cparams("parallel"),
        name="mix_sample",
    )(proj3, v_norm, proj3, proj3, proj3, convc_tm, lw["w_sub_e"], lw["b_sub_e"], lw["conv_c_w"])
    return yb, yc, cst, v_norm


def _merge_body(ya_ref, yb_ref, yc_ref, ga_ref, gb_ref, gc_ref, wa_ref, wb_ref, wc_ref, o_ref):
    merged = (_sigmoid(ga_ref[...]) * _dot(ya_ref[...], wa_ref[...])
              + _sigmoid(gb_ref[...]) * _dot(yb_ref[...], wb_ref[...])
              + _sigmoid(gc_ref[...]) * _dot(yc_ref[...], wc_ref[...]))
    o_ref[...] = merged.astype(BF16)


def _merge(ya, yb, yc, proj, d, ws, layer, cfg, *, tm, tn):
    rows = ya.shape[0]
    I, W, SCW = ya.shape[1], yb.shape[1], yc.shape[1]
    go = cfg["off"]["gate"] // tn
    act = lambda w: pl.BlockSpec((tm, w), lambda m, j: (m, 0))
    gate = lambda k: pl.BlockSpec((tm, tn), lambda m, j: (m, go + k * (d // tn) + j))
    wcol = lambda k: pl.BlockSpec((None, k, tn), lambda m, j: (layer, 0, j))
    return pl.pallas_call(
        _merge_body,
        grid=(rows // tm, d // tn),
        in_specs=[act(I), act(W), act(SCW), gate(0), gate(1), gate(2), wcol(I), wcol(W), wcol(SCW)],
        out_specs=pl.BlockSpec((tm, tn), lambda m, j: (m, j)),
        out_shape=jax.ShapeDtypeStruct((rows, d), BF16),
        compiler_params=_cparams("parallel", "parallel"),
        name="merge",
    )(ya, yb, yc, proj, proj, proj, ws["w_out_a"], ws["w_out_b"], ws["w_out_c"])


def _ffn_body(x_ref, mg_ref, wo_ref, nw_ref, wg_ref, wu_ref, wd_ref, fw_ref, o_ref, h_scr, *, sub, final):
    f = pl.program_id(1)
    tm = x_ref.shape[0]

    @pl.when(f == 0)
    def _():
        o_ref[...] = x_ref[...] + _dot(mg_ref[...], wo_ref[...])

        def rows(i, carry):
            r = pl.ds(pl.multiple_of(i * sub, sub), sub)
            h_scr[r, :] = _rms(o_ref[r, :], nw_ref[...]).astype(BF16)
            return carry
        lax.fori_loop(0, tm // sub, rows, 0)

    h = h_scr[...]
    act = _silu(_dot(h, wg_ref[...])) * _dot(h, wu_ref[...])
    o_ref[...] += _dot(act.astype(BF16), wd_ref[...])

    if final:
        @pl.when(f == pl.num_programs(1) - 1)
        def _():
            def rows(i, carry):
                r = pl.ds(pl.multiple_of(i * sub, sub), sub)
                o_ref[r, :] = _rms(o_ref[r, :], fw_ref[...])
                return carry
            lax.fori_loop(0, tm // sub, rows, 0)


def _ffn(x, merged, nw, ws, layer, final_w, *, tm, tf, final):
    rows, d = x.shape
    fh = ws["w_gate"].shape[2]
    act = pl.BlockSpec((tm, d), lambda m, f: (m, 0))
    vec = pl.BlockSpec((1, d), lambda m, f: (0, 0))
    wcol = pl.BlockSpec((None, d, tf), lambda m, f: (layer, 0, f))
    wo = pl.BlockSpec((None, d, d), lambda m, f: (layer, 0, 0), pipeline_mode=pl.Buffered(1))
    return pl.pallas_call(
        functools.partial(_ffn_body, sub=min(tm, 128), final=final),
        grid=(rows // tm, fh // tf),
        in_specs=[act, act, wo, vec, wcol, wcol, pl.BlockSpec((None, tf, d), lambda m, f: (layer, f, 0)), vec],
        out_specs=act,
        out_shape=jax.ShapeDtypeStruct((rows, d), F32),
        scratch_shapes=[pltpu.VMEM((tm, d), BF16)],
        compiler_params=_cparams("parallel", "arbitrary"),
        name="ffn",
    )(x, merged, ws["w_o"], nw, ws["w_gate"], ws["w_up"], ws["w_down"], final_w)


def _tile(n, pref, quantum=SUBLANES):
    for t in range(min(n, pref), 0, -1):
        if n % t == 0 and t % quantum == 0:
            return t
    raise ValueError((n, pref, quantum))


def _pad_rows(a, rows):
    a = jnp.swapaxes(a, 0, 1)
    return jnp.pad(a, ((0, 0), (0, rows - a.shape[1]), (0, 0)))


def kernel(x_prompt, x_sample, state_conv_a, state_ssm, state_conv_c, norm_mix_w, w_in, conv_a_w, conv_a_b,
           dt_bias, a_log, d_skip, ssm_norm_w, w_out_a, sgu_ln_w, sgu_ln_b, w_spatial, b_spatial, w_out_b,
           conv_c_w, w_out_c, w_o, norm_ffn_w, w_gate, w_up, w_down, norm_final_w):
    depth, d, _ = w_in.shape
    batch, seq, _ = x_prompt.shape
    sq, t_dec, _ = x_sample.shape
    I = ssm_norm_w.shape[1]
    H = dt_bias.shape[1]
    P, N = state_ssm.shape[3], state_ssm.shape[4]
    CD = conv_a_w.shape[2]
    G = (CD - I) // (2 * N)
    HG = H // G
    W = sgu_ln_w.shape[1]
    SG = w_spatial.shape[1]
    SCW = conv_c_w.shape[2]
    assert W == SCW and H <= LANES and LANES % P == 0 and (HG * P) % LANES == 0
    assert seq % CHUNK == 0 and t_dec <= SUBLANES and w_spatial.shape[2] == CHUNK

    sizes = (("z", I), ("x", I), ("bc", CD - I), ("u", W), ("v", W), ("scb", SCW), ("scc", SCW), ("sch", SCW),
             ("gate", 3 * d))
    off, o = {}, 0
    for name, s in sizes:
        off[name] = o
        o += s
    n_main = o
    cfg = dict(G=G, HG=HG, P=P, N=N, W=W, off=off)
    dt0 = I + CD

    w_in_t = jnp.swapaxes(w_in, 1, 2)
    w_in_main = _prep_w_in(w_in_t, dt0, H, n_main, _tile(math.gcd(dt0, n_main), 512, LANES))
    wdt_hi, wdt_lo = _prep_w_dt(w_in_t, dt0, H)
    ws = dict(w_out_a=w_out_a.astype(BF16), w_out_b=w_out_b.astype(BF16), w_out_c=w_out_c.astype(BF16),
              w_o=w_o.astype(BF16), w_gate=w_gate.astype(BF16), w_up=w_up.astype(BF16),
              w_down=w_down.astype(BF16))

    head_of_chan = jnp.arange(I) // P
    expand = (jnp.arange(LANES)[:, None] == head_of_chan[None, :]).astype(BF16)
    pos = (PAST_LEN + jnp.arange(t_dec)) % CHUNK
    padh = lambda v: jnp.pad(v, (0, LANES - H)).reshape(1, LANES)

    layers = []
    for i in range(depth):
        w_sub = jnp.tril(w_spatial[i])[:, pos[:, None], pos[None, :]]
        layers.append(dict(
            norm_mix_w=norm_mix_w[i].reshape(1, d), wdh=wdt_hi[i], wdl=wdt_lo[i],
            conv_a_w=conv_a_w[i], conv_a_b=conv_a_b[i].reshape(1, CD),
            dt_bias=padh(dt_bias[i]), a_log=padh(a_log[i]),
            d_skip_e=jnp.repeat(d_skip[i], P).reshape(1, I), ssm_norm_w=ssm_norm_w[i].reshape(1, I),
            expand=expand,
            sgu_ln_w=sgu_ln_w[i].reshape(1, W), sgu_ln_b=sgu_ln_b[i].reshape(1, W),
            w_spatial=w_spatial[i],
            b_spatial_e=jnp.repeat(b_spatial[i].T, W // SG, axis=1),
            w_sub_e=jnp.repeat(jnp.transpose(w_sub, (1, 2, 0)).reshape(t_dec * t_dec, SG), W // SG, axis=1),
            b_sub_e=jnp.repeat(b_spatial[i][:, pos].T, W // SG, axis=1),
            conv_c_w=conv_c_w[i],
            norm_ffn_w=norm_ffn_w[i].reshape(1, d),
        ))
    final_w = norm_final_w.reshape(1, d)

    rows_p = batch * seq
    rows_s = sq * t_dec
    yp = x_prompt.reshape(rows_p, d)
    ys = jnp.swapaxes(x_sample, 0, 1).reshape(rows_s, d)
    tn_in = _tile(n_main, 1024, LANES)
    fh = w_gate.shape[2]
    tf = _tile(fh, 512, LANES)
    state_all = state_ssm.reshape(depth, sq, I, N)

    ca_p, ssm_p, cc_p, ca_s, cc_s, v_s = [], [], [], [], [], []
    h_acc = None
    for i, lw in enumerate(layers):
        last = i == depth - 1
        proj, dt_raw = _inproj(yp, lw["norm_mix_w"], w_in_main, i, lw["wdh"], lw["wdl"],
                               tm=_tile(rows_p, 1024), tn=tn_in)
        ya, cst_a, h_new, yb, yc, cst_c = _mixers_prompt(proj, dt_raw, lw, cfg, batch, seq)
        mg = _merge(ya, yb, yc, proj, d, ws, i, cfg, tm=_tile(rows_p, 1024), tn=_tile(d, 256, LANES))
        yp = _ffn(yp, mg, lw["norm_ffn_w"], ws, i, final_w, tm=_tile(rows_p, 512), tf=tf, final=last)
        K = conv_a_w.shape[1]
        ca_p.append(cst_a[:, SUBLANES - (K - 1):, :])
        ssm_p.append(h_new.reshape(batch, H, P, N))
        cc_p.append(cst_c[:, SUBLANES - (conv_c_w.shape[1] - 1):, :])

        proj, dt_raw = _inproj(ys, lw["norm_mix_w"], w_in_main, i, lw["wdh"], lw["wdl"],
                               tm=_tile(rows_s, 512), tn=tn_in)
        proj3 = proj.reshape(t_dec, sq, n_main)
        dt3 = dt_raw.reshape(t_dec, sq, LANES)
        ypart, xdl, eae, bs, cs, cd, nx, nb, nc = _ssd_sample_a(
            proj3, dt3, jnp.swapaxes(state_conv_a[i], 0, 1), lw, cfg)
        ya_t, h_acc = _ssd_sample_b(
            cd[:, :H].reshape(sq * H), state_all, i, h_acc,
            _pad_rows(cs, BF16_ROWS), _pad_rows(bs, BF16_ROWS), _pad_rows(xdl.astype(BF16), BF16_ROWS),
            _pad_rows(eae, SUBLANES), _pad_rows(ypart, SUBLANES), _pad_rows(proj3[:, :, :I], SUBLANES),
            lw["ssm_norm_w"], cfg, _tile(sq, 32))
        ya = jnp.swapaxes(ya_t[:, :t_dec, :], 0, 1).reshape(rows_s, I).astype(BF16)
        yb, yc, cst_c, v_norm = _mix_sample(proj3, jnp.swapaxes(state_conv_c[i], 0, 1), lw, cfg,
                                            _tile(W, 512, LANES))
        mg = _merge(ya, yb.reshape(rows_s, W), yc.reshape(rows_s, SCW), proj, d, ws, i, cfg,
                    tm=_tile(rows_s, 1024), tn=_tile(d, 256, LANES))
        ys = _ffn(ys, mg, lw["norm_ffn_w"], ws, i, final_w, tm=_tile(rows_s, 512), tf=tf, final=last)
        ca_s.append(jnp.swapaxes(jnp.concatenate([nx, nb, nc], axis=-1), 0, 1))
        cc_s.append(jnp.swapaxes(cst_c, 0, 1))
        v_s.append(jnp.swapaxes(v_norm, 0, 1))

    y_prompt = yp.reshape(batch, seq, d)
    y_sample = jnp.swapaxes(ys.reshape(t_dec, sq, d), 0, 1)
    return (y_prompt, y_sample, jnp.stack(ca_p), jnp.stack(ssm_p), jnp.stack(cc_p),
            jnp.stack(ca_s), h_acc.reshape(depth, sq, H, P, N), jnp.stack(cc_s), jnp.stack(v_s))
```

```python
import functools
import math

import jax
import jax.numpy as jnp
from jax import lax
from jax.experimental import pallas as pl
from jax.experimental.pallas import tpu as pltpu

F32 = jnp.float32
BF16 = jnp.bfloat16

RMS_EPS = 1e-6
LN_EPS = 1e-5
PAST_LEN = 16384
CHUNK = 128
LANES = 128
SUBLANE_BITS = 3
SUBLANES = 1 << SUBLANE_BITS
BF16_ROWS = 16
NEG_BIG = -1e30
VMEM_LIMIT_BYTES = 56 * 1024 * 1024


def _cparams(*sem):
    return pltpu.CompilerParams(dimension_semantics=sem, vmem_limit_bytes=VMEM_LIMIT_BYTES)


def _dot(a, b):
    return jnp.dot(a, b, preferred_element_type=F32)


def _dot_nt(a, b):
    return lax.dot_general(a, b, (((1,), (1,)), ((), ())), preferred_element_type=F32)


def _dot_tn(a, b):
    return lax.dot_general(a, b, (((0,), (0,)), ((), ())), preferred_element_type=F32)


def _split3(x):
    hi = x.astype(BF16)
    r = x - hi.astype(F32)
    mid = r.astype(BF16)
    lo = (r - mid.astype(F32)).astype(BF16)
    return hi, mid, lo


def _dot_exact_rhs(x, m_bf16):
    hi, mid, lo = _split3(x)
    return _dot(hi, m_bf16) + _dot(mid, m_bf16) + _dot(lo, m_bf16)


def _dot_exact_lhs(m_bf16, x):
    hi, mid, lo = _split3(x)
    return _dot(m_bf16, hi) + _dot(m_bf16, mid) + _dot(m_bf16, lo)


def _sigmoid(x):
    return 0.5 * jnp.tanh(0.5 * x) + 0.5


def _silu(x):
    return x * _sigmoid(x)


def _softplus(x):
    return jnp.maximum(x, 0.0) + jnp.log1p(jnp.exp(-jnp.abs(x)))


def _rms(x, w):
    return x * lax.rsqrt(jnp.mean(x * x, axis=-1, keepdims=True) + RMS_EPS) * w


def _causal_taps_interleaved(v_tiles, w, carry_ref, cs):
    k = w.shape[0]
    sub = lax.broadcasted_iota(jnp.int32, v_tiles[0].shape, 0)
    acc = [v * w[0:1, :] for v in v_tiles]
    for j in range(1, k):
        prev_last = carry_ref[j - 1, :, cs]
        carry_ref[j - 1, :, cs] = acc[-1]
        first = jnp.where(sub == 0, pltpu.roll(prev_last, 1, axis=0), pltpu.roll(acc[-1], 1, axis=0))
        acc = [first] + acc[:-1]
        acc = [a + v * w[j:j + 1, :] for a, v in zip(acc, v_tiles)]
    return jnp.concatenate(acc, axis=0)


def _layernorm(x, w, b):
    mu = jnp.mean(x, axis=-1, keepdims=True)
    d = x - mu
    var = jnp.mean(d * d, axis=-1, keepdims=True)
    return d * lax.rsqrt(var + LN_EPS) * w + b


def _inproj_body(x_ref, nw_ref, w_ref, wdh_ref, wdl_ref, o_ref, dt_ref, h_scr, *, sub):
    tm = x_ref.shape[0]

    @pl.when(pl.program_id(1) == 0)
    def _():
        def rows(i, carry):
            r = pl.ds(pl.multiple_of(i * sub, sub), sub)
            h = _rms(x_ref[r, :], nw_ref[...])
            hb = h.astype(BF16)
            h_scr[r, :] = hb
            hl = (h - hb.astype(F32)).astype(BF16)
            dt_ref[r, :] = _dot(hb, wdh_ref[...]) + _dot(hb, wdl_ref[...]) + _dot(hl, wdh_ref[...])
            return carry
        lax.fori_loop(0, tm // sub, rows, 0)

    o_ref[...] = _dot(h_scr[...], w_ref[...])


def _prep_w_in_body(a_ref, b_ref, o_ref, *, j0, shift, sub):
    tw, d = a_ref.shape
    j = pl.program_id(1)

    @pl.when(j < j0)
    def _():
        for k in range(d // sub):
            ks = slice(k * sub, (k + 1) * sub)
            o_ref[ks, :] = a_ref[:, ks].T.astype(BF16)

    @pl.when(j >= j0)
    def _():
        for k in range(d // sub):
            ks = slice(k * sub, (k + 1) * sub)
            src = jnp.concatenate([a_ref[shift:, ks], b_ref[:, ks]], axis=0)
            o_ref[ks, :] = src.T.astype(BF16)


def _prep_w_in(w_in_t, dt0, shift, n_main, tw):
    depth, _, d = w_in_t.shape
    assert dt0 % tw == 0 and n_main % tw == 0 and tw % shift == 0 and shift % SUBLANES == 0
    j0 = dt0 // tw
    per = tw // shift
    return pl.pallas_call(
        functools.partial(_prep_w_in_body, j0=j0, shift=shift, sub=min(d, 256)),
        grid=(depth, n_main // tw),
        in_specs=[
            pl.BlockSpec((None, tw, d), lambda l, j: (l, j, 0)),
            pl.BlockSpec((None, shift, d), lambda l, j: (l, jnp.where(j < j0, 0, per * (j + 1)), 0)),
        ],
        out_specs=pl.BlockSpec((None, d, tw), lambda l, j: (l, 0, j)),
        out_shape=jax.ShapeDtypeStruct((depth, d, n_main), BF16),
        compiler_params=_cparams("parallel", "parallel"),
        name="prep_w_in",
    )(w_in_t, w_in_t)


def _prep_w_dt_body(a_ref, hi_ref, lo_ref, *, sub):
    h, d = a_ref.shape
    for k in range(d // sub):
        ks = slice(k * sub, (k + 1) * sub)
        src = jnp.concatenate([a_ref[:, ks], jnp.zeros((LANES - h, sub), F32)], axis=0)
        w = src.T
        hi = w.astype(BF16)
        hi_ref[ks, :] = hi
        lo_ref[ks, :] = (w - hi.astype(F32)).astype(BF16)


def _prep_w_dt(w_in_t, dt0, h):
    depth, _, d = w_in_t.shape
    assert dt0 % h == 0 and h % SUBLANES == 0 and h <= LANES
    out = pl.BlockSpec((None, d, LANES), lambda l: (l, 0, 0))
    return pl.pallas_call(
        functools.partial(_prep_w_dt_body, sub=min(d, 256)),
        grid=(depth,),
        in_specs=[pl.BlockSpec((None, h, d), lambda l: (l, dt0 // h, 0))],
        out_specs=[out, out],
        out_shape=[jax.ShapeDtypeStruct((depth, d, LANES), BF16)] * 2,
        compiler_params=_cparams("parallel"),
        name="prep_w_dt",
    )(w_in_t)


def _inproj(x, nw, w, layer, wdh, wdl, *, tm, tn):
    rows, d = x.shape
    n = w.shape[2]
    return pl.pallas_call(
        functools.partial(_inproj_body, sub=min(tm, 128)),
        grid=(rows // tm, n // tn),
        in_specs=[
            pl.BlockSpec((tm, d), lambda m, j: (m, 0)),
            pl.BlockSpec((1, d), lambda m, j: (0, 0)),
            pl.BlockSpec((None, d, tn), lambda m, j: (layer, 0, j)),
            pl.BlockSpec((d, LANES), lambda m, j: (0, 0)),
            pl.BlockSpec((d, LANES), lambda m, j: (0, 0)),
        ],
        out_specs=[
            pl.BlockSpec((tm, tn), lambda m, j: (m, j)),
            pl.BlockSpec((tm, LANES), lambda m, j: (m, 0)),
        ],
        out_shape=[jax.ShapeDtypeStruct((rows, n), F32), jax.ShapeDtypeStruct((rows, LANES), F32)],
        scratch_shapes=[pltpu.VMEM((tm, d), BF16)],
        compiler_params=_cparams("parallel", "arbitrary"),
        name="inproj",
    )(x, nw, w, wdh, wdl)


def _mixers_prompt_body(z_ref, x_ref, bc_ref, dt_ref, cw_ref, cb_ref, dtb_ref, alog_ref, dsk_ref, nw_ref,
                        e_ref, u_ref, v_ref, sb_ref, sc_ref, sh_ref, lnw_ref, lnb_ref, ws_ref, be_ref, ccw_ref,
                        ya_ref, cst_ref, h_ref, yb_ref, yc_ref, cstc_ref, cx, cbc, cq, vb_scr,
                        *, G, HG, P, N):
    c = pl.program_id(1)
    L = CHUNK
    NT = L // SUBLANES
    GW = HG * P
    I = G * GW
    W = u_ref.shape[1]
    SG = ws_ref.shape[0]
    GWS = W // SG
    K = cw_ref.shape[0]
    KC = ccw_ref.shape[0]

    @pl.when(c == 0)
    def _():
        h_ref[...] = jnp.zeros(h_ref.shape, F32)
        cx[...] = jnp.zeros(cx.shape, F32)
        cbc[...] = jnp.zeros(cbc.shape, F32)
        cq[...] = jnp.zeros(cq.shape, F32)

    vb_scr[...] = _layernorm(jax.nn.gelu(v_ref[...]), lnw_ref[...], lnb_ref[...]).astype(BF16)

    def tile(ref, k, cols):
        return ref[k * SUBLANES:(k + 1) * SUBLANES, cols]

    def gmlp_and_short_conv(g):
        gs = slice(g * GWS, (g + 1) * GWS)
        mixed = _dot(ws_ref[g].astype(BF16), vb_scr[:, gs]) + be_ref[:, gs]
        yb_ref[:, gs] = (jax.nn.gelu(u_ref[:, gs]) * mixed).astype(BF16)
        q = [tile(sc_ref, k, gs) * tile(sh_ref, k, gs) for k in range(NT)]
        conv_c = _causal_taps_interleaved(q, ccw_ref[:, gs], cq, gs)
        yc_ref[:, gs] = (sb_ref[:, gs] * conv_c).astype(BF16)
        for i in range(KC - 1):
            cstc_ref[0, i * SUBLANES:(i + 1) * SUBLANES, gs] = q[NT - (KC - 1) + i]

    def conv(ref, carry, lo, hi, col0):
        cs = slice(lo, hi)
        acc = _causal_taps_interleaved([tile(ref, k, cs) for k in range(NT)],
                                       cw_ref[:, col0 + lo:col0 + hi], carry, cs)
        return _silu(acc + cb_ref[:, col0 + lo:col0 + hi])

    dt = _softplus(dt_ref[...] + dtb_ref[...])
    a = -jnp.exp(alog_ref[...])
    row = lax.broadcasted_iota(jnp.int32, (L, L), 0)
    col = lax.broadcasted_iota(jnp.int32, (L, L), 1)
    time_of = lambda r: (r & (SUBLANES - 1)) * NT + (r >> SUBLANE_BITS)
    tri = time_of(row) >= time_of(col)
    tril_ones = jnp.where(tri, 1.0, 0.0).astype(BF16)
    acum = _dot_exact_lhs(tril_ones, dt * a)
    acum_t = acum.T
    last = acum[L - 1:L, :]
    dl = jnp.exp(last - acum)
    ea = jnp.exp(acum)
    cd_t = jnp.exp(acum_t[:, L - 1:L])
    dt_s, dl_s, ea_s = _split3(dt), _split3(dl), _split3(ea)
    lane = lax.broadcasted_iota(jnp.int32, (L, LANES), 1)
    HPL = LANES // P

    for g in range(G):
        gs = slice(g * GW, (g + 1) * GW)
        e_g = e_ref[:, gs]

        def expand(parts):
            return _dot(parts[0], e_g) + _dot(parts[1], e_g) + _dot(parts[2], e_g)

        xg = conv(x_ref, cx, g * GW, (g + 1) * GW, 0)
        xdt = xg * expand(dt_s)
        bb = conv(bc_ref, cbc, g * N, (g + 1) * N, I).astype(BF16)
        cb16 = conv(bc_ref, cbc, (G + g) * N, (G + g + 1) * N, I).astype(BF16)
        cbm = _dot_nt(cb16, bb)
        hg16 = h_ref[0, gs, :].astype(BF16)
        y = _dot_nt(cb16, hg16) * expand(ea_s) + dsk_ref[:, gs] * xg

        tiles = []
        for j in range(GW // LANES):
            xt = xdt[:, j * LANES:(j + 1) * LANES]
            acc = jnp.zeros((L, LANES), F32)
            for hh in range(HPL):
                h = g * HG + j * HPL + hh
                seg = acum[:, h:h + 1] - acum_t[h:h + 1, :]
                dec = jnp.exp(jnp.where(tri, seg, NEG_BIG))
                m = (cbm * dec).astype(BF16)
                inhead = (lane >= hh * P) & (lane < (hh + 1) * P)
                acc = acc + _dot(m, jnp.where(inhead, xt, 0.0).astype(BF16))
            tiles.append(acc)
        y = y + jnp.concatenate(tiles, axis=1)

        gt = y * _silu(z_ref[:, gs])
        ya_ref[:, gs] = _rms(gt, nw_ref[:, gs]).astype(BF16)

        s_new = _dot_tn((xdt * expand(dl_s)).astype(BF16), bb)
        for hh in range(HG):
            h = g * HG + hh
            rs = slice(g * GW + hh * P, g * GW + (hh + 1) * P)
            h_ref[0, rs, :] = h_ref[0, rs, :] * cd_t[h:h + 1, :] + s_new[hh * P:(hh + 1) * P, :]

        if g < SG:
            gmlp_and_short_conv(g)
    for g in range(G, SG):
        gmlp_and_short_conv(g)

    tail = slice((NT - (K - 1)) * SUBLANES, L)
    cst_ref[0, :, 0:I] = x_ref[tail, :]
    cst_ref[0, :, I:] = bc_ref[tail, :]


def _mixers_prompt(proj, dt_raw, lw, cfg, batch, seq):
    G, HG, P, N, W = cfg["G"], cfg["HG"], cfg["P"], cfg["N"], cfg["W"]
    I = G * HG * P
    BC = 2 * G * N
    CD = I + BC
    nc = seq // CHUNK
    off = cfg["off"]
    ka = (lw["conv_a_w"].shape[0] - 1) * SUBLANES
    kc = (lw["conv_c_w"].shape[0] - 1) * SUBLANES
    row = lambda b, c: b * nc + c
    full = lambda b, c: (0, 0)
    tok = lambda name: pl.BlockSpec((CHUNK, W), lambda b, c: (row(b, c), off[name] // W))
    return pl.pallas_call(
        functools.partial(_mixers_prompt_body, G=G, HG=HG, P=P, N=N),
        grid=(batch, nc),
        in_specs=[
            pl.BlockSpec((CHUNK, I), lambda b, c: (row(b, c), off["z"] // I)),
            pl.BlockSpec((CHUNK, I), lambda b, c: (row(b, c), off["x"] // I)),
            pl.BlockSpec((CHUNK, BC), lambda b, c: (row(b, c), off["bc"] // BC)),
            pl.BlockSpec((CHUNK, LANES), lambda b, c: (row(b, c), 0)),
            pl.BlockSpec(lw["conv_a_w"].shape, full),
            pl.BlockSpec((1, CD), full),
            pl.BlockSpec((1, LANES), full),
            pl.BlockSpec((1, LANES), full),
            pl.BlockSpec((1, I), full),
            pl.BlockSpec((1, I), full),
            pl.BlockSpec((LANES, I), full),
            tok("u"), tok("v"), tok("scb"), tok("scc"), tok("sch"),
            pl.BlockSpec((1, W), full), pl.BlockSpec((1, W), full),
            pl.BlockSpec(lw["w_spatial"].shape, lambda b, c: (0, 0, 0)),
            pl.BlockSpec((CHUNK, W), full),
            pl.BlockSpec(lw["conv_c_w"].shape, full),
        ],
        out_specs=[
            pl.BlockSpec((CHUNK, I), lambda b, c: (row(b, c), 0)),
            pl.BlockSpec((1, ka, CD), lambda b, c: (b, 0, 0)),
            pl.BlockSpec((1, I, N), lambda b, c: (b, 0, 0)),
            pl.BlockSpec((CHUNK, W), lambda b, c: (row(b, c), 0)),
            pl.BlockSpec((CHUNK, W), lambda b, c: (row(b, c), 0)),
            pl.BlockSpec((1, kc, W), lambda b, c: (b, 0, 0)),
        ],
        out_shape=[
            jax.ShapeDtypeStruct((batch * seq, I), BF16),
            jax.ShapeDtypeStruct((batch, ka, CD), F32),
            jax.ShapeDtypeStruct((batch, I, N), F32),
            jax.ShapeDtypeStruct((batch * seq, W), BF16),
            jax.ShapeDtypeStruct((batch * seq, W), BF16),
            jax.ShapeDtypeStruct((batch, kc, W), F32),
        ],
        scratch_shapes=[
            pltpu.VMEM((ka // SUBLANES, SUBLANES, I), F32),
            pltpu.VMEM((ka // SUBLANES, SUBLANES, BC), F32),
            pltpu.VMEM((kc // SUBLANES, SUBLANES, W), F32),
            pltpu.VMEM((CHUNK, W), BF16),
        ],
        compiler_params=_cparams("parallel", "arbitrary"),
        name="mixers_prompt",
    )(proj, proj, proj, dt_raw, lw["conv_a_w"], lw["conv_a_b"], lw["dt_bias"], lw["a_log"],
      lw["d_skip_e"], lw["ssm_norm_w"], lw["expand"],
      proj, proj, proj, proj, proj, lw["sgu_ln_w"], lw["sgu_ln_b"], lw["w_spatial"], lw["b_spatial_e"],
      lw["conv_c_w"])


def _ssd_sample_a_body(x_ref, b_ref, c_ref, dt_ref, px_ref, pb_ref, pc_ref, cwx_ref, cwb_ref, cwc_ref,
                       cbx_ref, cbb_ref, cbc_ref, dtb_ref, alog_ref, dsk_ref, e_ref,
                       yp_ref, xdl_ref, ea_ref, bs_ref, cs_ref, cd_ref, nx_ref, nb_ref, nc_ref):
    T = x_ref.shape[0]
    K = cwx_ref.shape[0]

    def conv(p_ref, cur_ref, w_ref, bias_ref, new_ref):
        rows = [p_ref[j] for j in range(K - 1)] + [cur_ref[l] for l in range(T)]
        for j in range(K - 1):
            new_ref[j] = rows[T + j]
        outs = []
        for l in range(T):
            acc = rows[l] * w_ref[0:1, :]
            for j in range(1, K):
                acc = acc + rows[l + j] * w_ref[j:j + 1, :]
            outs.append(_silu(acc + bias_ref[...]))
        return outs

    xs = conv(px_ref, x_ref, cwx_ref, cbx_ref, nx_ref)
    bs = conv(pb_ref, b_ref, cwb_ref, cbb_ref, nb_ref)
    cs = conv(pc_ref, c_ref, cwc_ref, cbc_ref, nc_ref)

    a = -jnp.exp(alog_ref[...])
    dts, acums = [], []
    run = None
    for l in range(T):
        dt = _softplus(dt_ref[l] + dtb_ref[...])
        run = dt * a if run is None else run + dt * a
        dts.append(dt)
        acums.append(run)
    last = acums[T - 1]
    cd_ref[...] = jnp.exp(last)

    e_g = e_ref[...]

    def expand(coef):
        return _dot_exact_rhs(coef, e_g)

    xdts = [xs[l] * expand(dts[l]) for l in range(T)]
    for l in range(T):
        bs_ref[l] = bs[l].astype(BF16)
        cs_ref[l] = cs[l].astype(BF16)
        ea_ref[l] = expand(jnp.exp(acums[l]))
        xdl_ref[l] = xdts[l] * expand(jnp.exp(last - acums[l]))
        y = dsk_ref[...] * xs[l]
        for s in range(l + 1):
            cb = jnp.sum(cs[l] * bs[s], axis=-1, keepdims=True)
            y = y + expand(cb * jnp.exp(acums[l] - acums[s])) * xdts[s]
        yp_ref[l] = y


def _ssd_sample_a(proj3, dt3, conv_tm, lw, cfg):
    G, HG, P, N = cfg["G"], cfg["HG"], cfg["P"], cfg["N"]
    GW = HG * P
    I = G * GW
    T, SQ, _ = proj3.shape
    K = lw["conv_a_w"].shape[0]
    off = cfg["off"]
    xo, bo, co = off["x"] // GW, off["bc"] // N, off["bc"] // N + G
    sxo, sbo, sco = 0, I // N, I // N + G
    tok = lambda w, o: pl.BlockSpec((T, SQ, w), lambda g: (0, 0, o + g))
    pre = lambda w, o: pl.BlockSpec((K - 1, SQ, w), lambda g: (0, 0, o + g))
    par = lambda r, w, o: pl.BlockSpec((r, w), lambda g: (0, o + g))
    fixed = lambda shape: pl.BlockSpec(shape, lambda g: tuple(0 for _ in shape))
    return pl.pallas_call(
        _ssd_sample_a_body,
        grid=(G,),
        in_specs=[
            tok(GW, xo), tok(N, bo), tok(N, co), fixed((T, SQ, LANES)),
            pre(GW, sxo), pre(N, sbo), pre(N, sco),
            par(K, GW, sxo), par(K, N, sbo), par(K, N, sco),
            par(1, GW, sxo), par(1, N, sbo), par(1, N, sco),
            fixed((1, LANES)), fixed((1, LANES)), par(1, GW, 0), par(LANES, GW, 0),
        ],
        out_specs=[
            tok(GW, 0), tok(GW, 0), tok(GW, 0), tok(N, 0), tok(N, 0), fixed((SQ, LANES)),
            pre(GW, 0), pre(N, 0), pre(N, 0),
        ],
        out_shape=[
            jax.ShapeDtypeStruct((T, SQ, I), F32),
            jax.ShapeDtypeStruct((T, SQ, I), F32),
            jax.ShapeDtypeStruct((T, SQ, I), F32),
            jax.ShapeDtypeStruct((T, SQ, G * N), BF16),
            jax.ShapeDtypeStruct((T, SQ, G * N), BF16),
            jax.ShapeDtypeStruct((SQ, LANES), F32),
            jax.ShapeDtypeStruct((K - 1, SQ, I), F32),
            jax.ShapeDtypeStruct((K - 1, SQ, G * N), F32),
            jax.ShapeDtypeStruct((K - 1, SQ, G * N), F32),
        ],
        compiler_params=_cparams("arbitrary"),
        name="ssd_sample_a",
    )(proj3, proj3, proj3, dt3, conv_tm, conv_tm, conv_tm,
      lw["conv_a_w"], lw["conv_a_w"], lw["conv_a_w"], lw["conv_a_b"], lw["conv_a_b"], lw["conv_a_b"],
      lw["dt_bias"], lw["a_log"], lw["d_skip_e"], lw["expand"])


def _ssd_sample_b_body(cd_ref, h0_ref, cs_ref, bs_ref, xdl_ref, ea_ref, yp_ref, z_ref, nw_ref,
                       *rest, HG, P, H):
    ya_ref, hn_ref = rest[-2:]
    sb = pl.program_id(0)
    g = pl.program_id(1)
    SB = h0_ref.shape[0]
    TP = yp_ref.shape[1]
    for i in range(SB):
        hmat = h0_ref[i]
        yoff = _dot_nt(cs_ref[i], hmat.astype(BF16))
        y = yp_ref[i] + yoff[0:TP, :] * ea_ref[i]
        gt = y * _silu(z_ref[i])
        ya_ref[i] = _rms(gt, nw_ref[...])
        s_new = _dot_tn(xdl_ref[i], bs_ref[i])
        for hh in range(HG):
            cd = cd_ref[(sb * SB + i) * H + g * HG + hh]
            rs = slice(hh * P, (hh + 1) * P)
            hn_ref[i, rs, :] = hmat[rs, :] * cd + s_new[rs, :]


def _ssd_sample_b(cd_flat, h0_all, layer, h_acc, cs_t, bs_t, xdl_t, ea_t, yp_t, z_t, nw, cfg, sb):
    G, HG, P, N = cfg["G"], cfg["HG"], cfg["P"], cfg["N"]
    GW = HG * P
    _, SQ, I, _ = h0_all.shape
    TP = yp_t.shape[1]
    tok = lambda r, w: pl.BlockSpec((sb, r, w), lambda s, g: (s, 0, g))
    state = pl.BlockSpec((None, sb, GW, N), lambda s, g: (layer, s, g, 0))
    in_specs = [
        pl.BlockSpec(memory_space=pltpu.SMEM),
        state,
        tok(BF16_ROWS, N), tok(BF16_ROWS, N), tok(BF16_ROWS, GW),
        tok(TP, GW), tok(TP, GW), tok(TP, GW),
        pl.BlockSpec((1, GW), lambda s, g: (0, g)),
    ]
    args = [cd_flat, h0_all, cs_t, bs_t, xdl_t, ea_t, yp_t, z_t, nw]
    aliases = {}
    if h_acc is not None:
        in_specs.append(pl.BlockSpec(memory_space=pl.ANY))
        args.append(h_acc)
        aliases = {len(args) - 1: 1}
    return pl.pallas_call(
        functools.partial(_ssd_sample_b_body, HG=HG, P=P, H=G * HG),
        grid=(SQ // sb, G),
        in_specs=in_specs,
        out_specs=[tok(TP, GW), state],
        out_shape=[
            jax.ShapeDtypeStruct((SQ, TP, I), F32),
            jax.ShapeDtypeStruct(h0_all.shape, F32),
        ],
        input_output_aliases=aliases,
        compiler_params=_cparams("parallel", "parallel"),
        name="ssd_sample_b",
    )(*args)


def _mix_sample_stats_body(v_ref, lnw_ref, lnb_ref, vo_ref):
    for l in range(v_ref.shape[0]):
        vo_ref[l] = _layernorm(jax.nn.gelu(v_ref[l]), lnw_ref[...], lnb_ref[...])


def _mix_sample_body(u_ref, v_ref, sb_ref, sc_ref, sh_ref, pq_ref, we_ref, be_ref, cw_ref,
                     yb_ref, yc_ref, cst_ref):
    T = u_ref.shape[0]
    K = cw_ref.shape[0]
    vs = [v_ref[l] for l in range(T)]
    for l in range(T):
        mixed = be_ref[l:l + 1, :]
        for s in range(T):
            mixed = mixed + we_ref[l * T + s:l * T + s + 1, :] * vs[s]
        yb_ref[l] = (jax.nn.gelu(u_ref[l]) * mixed).astype(BF16)
    rows = [pq_ref[j] for j in range(K - 1)] + [sc_ref[l] * sh_ref[l] for l in range(T)]
    for j in range(K - 1):
        cst_ref[j] = rows[T + j]
    for l in range(T):
        conv = rows[l] * cw_ref[0:1, :]
        for j in range(1, K):
            conv = conv + rows[l + j] * cw_ref[j:j + 1, :]
        yc_ref[l] = (sb_ref[l] * conv).astype(BF16)


def _mix_sample(proj3, convc_tm, lw, cfg, wc):
    W = cfg["W"]
    off = cfg["off"]
    T, SQ, _ = proj3.shape
    K = lw["conv_c_w"].shape[0]
    v_norm = pl.pallas_call(
        _mix_sample_stats_body,
        grid=(1,),
        in_specs=[
            pl.BlockSpec((T, SQ, W), lambda i: (0, 0, off["v"] // W)),
            pl.BlockSpec((1, W), lambda i: (0, 0)),
            pl.BlockSpec((1, W), lambda i: (0, 0)),
        ],
        out_specs=pl.BlockSpec((T, SQ, W), lambda i: (0, 0, 0)),
        out_shape=jax.ShapeDtypeStruct((T, SQ, W), F32),
        compiler_params=_cparams("arbitrary"),
        name="mix_sample_norm",
    )(proj3, lw["sgu_ln_w"], lw["sgu_ln_b"])
    tok = lambda name: pl.BlockSpec((T, SQ, wc), lambda j: (0, 0, off[name] // wc + j))
    loc = lambda r: pl.BlockSpec((r, SQ, wc), lambda j: (0, 0, j))
    par = lambda r: pl.BlockSpec((r, wc), lambda j: (0, j))
    yb, yc, cst = pl.pallas_call(
        _mix_sample_body,
        grid=(W // wc,),
        in_specs=[tok("u"), loc(T), tok("scb"), tok("scc"), tok("sch"), loc(K - 1),
                  par(T * T), par(T), par(K)],
        out_specs=[loc(T), loc(T), loc(K - 1)],
        out_shape=[
            jax.ShapeDtypeStruct((T, SQ, W), BF16),
            jax.ShapeDtypeStruct((T, SQ, W), BF16),
            jax.ShapeDtypeStruct((K - 1, SQ, W), F32),
        ],
        compiler_params=_cparams("parallel"),
        name="mix_sample",
    )(proj3, v_norm, proj3, proj3, proj3, convc_tm, lw["w_sub_e"], lw["b_sub_e"], lw["conv_c_w"])
    return yb, yc, cst, v_norm


def _merge_body(ya_ref, yb_ref, yc_ref, ga_ref, gb_ref, gc_ref, wa_ref, wb_ref, wc_ref, o_ref):
    merged = (_sigmoid(ga_ref[...]) * _dot(ya_ref[...], wa_ref[...])
              + _sigmoid(gb_ref[...]) * _dot(yb_ref[...], wb_ref[...])
              + _sigmoid(gc_ref[...]) * _dot(yc_ref[...], wc_ref[...]))
    o_ref[...] = merged.astype(BF16)


def _merge(ya, yb, yc, proj, d, ws, layer, cfg, *, tm, tn):
    rows = ya.shape[0]
    I, W, SCW = ya.shape[1], yb.shape[1], yc.shape[1]
    go = cfg["off"]["gate"] // tn
    act = lambda w: pl.BlockSpec((tm, w), lambda m, j: (m, 0))
    gate = lambda k: pl.BlockSpec((tm, tn), lambda m, j: (m, go + k * (d // tn) + j))
    wcol = lambda k: pl.BlockSpec((None, k, tn), lambda m, j: (layer, 0, j))
    return pl.pallas_call(
        _merge_body,
        grid=(rows // tm, d // tn),
        in_specs=[act(I), act(W), act(SCW), gate(0), gate(1), gate(2), wcol(I), wcol(W), wcol(SCW)],
        out_specs=pl.BlockSpec((tm, tn), lambda m, j: (m, j)),
        out_shape=jax.ShapeDtypeStruct((rows, d), BF16),
        compiler_params=_cparams("parallel", "parallel"),
        name="merge",
    )(ya, yb, yc, proj, proj, proj, ws["w_out_a"], ws["w_out_b"], ws["w_out_c"])


def _ffn_body(x_ref, mg_ref, wo_ref, nw_ref, wg_ref, wu_ref, wd_ref, fw_ref, o_ref, h_scr, *, sub, final):
    f = pl.program_id(1)
    tm = x_ref.shape[0]

    @pl.when(f == 0)
    def _():
        o_ref[...] = x_ref[...] + _dot(mg_ref[...], wo_ref[...])

        def rows(i, carry):
            r = pl.ds(pl.multiple_of(i * sub, sub), sub)
            h_scr[r, :] = _rms(o_ref[r, :], nw_ref[...]).astype(BF16)
            return carry
        lax.fori_loop(0, tm // sub, rows, 0)

    h = h_scr[...]
    act = _silu(_dot(h, wg_ref[...])) * _dot(h, wu_ref[...])
    o_ref[...] += _dot(act.astype(BF16), wd_ref[...])

    if final:
        @pl.when(f == pl.num_programs(1) - 1)
        def _():
            def rows(i, carry):
                r = pl.ds(pl.multiple_of(i * sub, sub), sub)
                o_ref[r, :] = _rms(o_ref[r, :], fw_ref[...])
                return carry
            lax.fori_loop(0, tm // sub, rows, 0)


def _ffn(x, merged, nw, ws, layer, final_w, *, tm, tf, final):
    rows, d = x.shape
    fh = ws["w_gate"].shape[2]
    act = pl.BlockSpec((tm, d), lambda m, f: (m, 0))
    vec = pl.BlockSpec((1, d), lambda m, f: (0, 0))
    wcol = pl.BlockSpec((None, d, tf), lambda m, f: (layer, 0, f))
    wo = pl.BlockSpec((None, d, d), lambda m, f: (layer, 0, 0), pipeline_mode=pl.Buffered(1))
    return pl.pallas_call(
        functools.partial(_ffn_body, sub=min(tm, 128), final=final),
        grid=(rows // tm, fh // tf),
        in_specs=[act, act, wo, vec, wcol, wcol, pl.BlockSpec((None, tf, d), lambda m, f: (layer, f, 0)), vec],
        out_specs=act,
        out_shape=jax.ShapeDtypeStruct((rows, d), F32),
        scratch_shapes=[pltpu.VMEM((tm, d), BF16)],
        compiler_params=_cparams("parallel", "arbitrary"),
        name="ffn",
    )(x, merged, ws["w_o"], nw, ws["w_gate"], ws["w_up"], ws["w_down"], final_w)


def _tile(n, pref, quantum=SUBLANES):
    for t in range(min(n, pref), 0, -1):
        if n % t == 0 and t % quantum == 0:
            return t
    raise ValueError((n, pref, quantum))


def _pad_rows(a, rows):
    a = jnp.swapaxes(a, 0, 1)
    return jnp.pad(a, ((0, 0), (0, rows - a.shape[1]), (0, 0)))


def kernel(x_prompt, x_sample, state_conv_a, state_ssm, state_conv_c, norm_mix_w, w_in, conv_a_w, conv_a_b,
           dt_bias, a_log, d_skip, ssm_norm_w, w_out_a, sgu_ln_w, sgu_ln_b, w_spatial, b_spatial, w_out_b,
           conv_c_w, w_out_c, w_o, norm_ffn_w, w_gate, w_up, w_down, norm_final_w):
    depth, d, _ = w_in.shape
    batch, seq, _ = x_prompt.shape
    sq, t_dec, _ = x_sample.shape
    I = ssm_norm_w.shape[1]
    H = dt_bias.shape[1]
    P, N = state_ssm.shape[3], state_ssm.shape[4]
    CD = conv_a_w.shape[2]
    G = (CD - I) // (2 * N)
    HG = H // G
    W = sgu_ln_w.shape[1]
    SG = w_spatial.shape[1]
    SCW = conv_c_w.shape[2]
    assert W == SCW and H <= LANES and LANES % P == 0 and (HG * P) % LANES == 0
    assert seq % CHUNK == 0 and t_dec <= SUBLANES and w_spatial.shape[2] == CHUNK

    sizes = (("z", I), ("x", I), ("bc", CD - I), ("u", W), ("v", W), ("scb", SCW), ("scc", SCW), ("sch", SCW),
             ("gate", 3 * d))
    off, o = {}, 0
    for name, s in sizes:
        off[name] = o
        o += s
    n_main = o
    cfg = dict(G=G, HG=HG, P=P, N=N, W=W, off=off)
    dt0 = I + CD

    w_in_t = jnp.swapaxes(w_in, 1, 2)
    w_in_main = _prep_w_in(w_in_t, dt0, H, n_main, _tile(math.gcd(dt0, n_main), 512, LANES))
    wdt_hi, wdt_lo = _prep_w_dt(w_in_t, dt0, H)
    ws = dict(w_out_a=w_out_a.astype(BF16), w_out_b=w_out_b.astype(BF16), w_out_c=w_out_c.astype(BF16),
              w_o=w_o.astype(BF16), w_gate=w_gate.astype(BF16), w_up=w_up.astype(BF16),
              w_down=w_down.astype(BF16))

    head_of_chan = jnp.arange(I) // P
    expand = (jnp.arange(LANES)[:, None] == head_of_chan[None, :]).astype(BF16)
    pos = (PAST_LEN + jnp.arange(t_dec)) % CHUNK
    NT = CHUNK // SUBLANES
    time_of_row = (jnp.arange(CHUNK) % SUBLANES) * NT + jnp.arange(CHUNK) // SUBLANES
    padh = lambda v: jnp.pad(v, (0, LANES - H)).reshape(1, LANES)

    layers = []
    for i in range(depth):
        w_sub = jnp.tril(w_spatial[i])[:, pos[:, None], pos[None, :]]
        layers.append(dict(
            norm_mix_w=norm_mix_w[i].reshape(1, d), wdh=wdt_hi[i], wdl=wdt_lo[i],
            conv_a_w=conv_a_w[i], conv_a_b=conv_a_b[i].reshape(1, CD),
            dt_bias=padh(dt_bias[i]), a_log=padh(a_log[i]),
            d_skip_e=jnp.repeat(d_skip[i], P).reshape(1, I), ssm_norm_w=ssm_norm_w[i].reshape(1, I),
            expand=expand,
            sgu_ln_w=sgu_ln_w[i].reshape(1, W), sgu_ln_b=sgu_ln_b[i].reshape(1, W),
            w_spatial=jnp.tril(w_spatial[i])[:, time_of_row[:, None], time_of_row[None, :]],
            b_spatial_e=jnp.repeat(b_spatial[i].T[time_of_row], W // SG, axis=1),
            w_sub_e=jnp.repeat(jnp.transpose(w_sub, (1, 2, 0)).reshape(t_dec * t_dec, SG), W // SG, axis=1),
            b_sub_e=jnp.repeat(b_spatial[i][:, pos].T, W // SG, axis=1),
            conv_c_w=conv_c_w[i],
            norm_ffn_w=norm_ffn_w[i].reshape(1, d),
        ))
    final_w = norm_final_w.reshape(1, d)

    rows_p = batch * seq
    rows_s = sq * t_dec
    yp = jnp.swapaxes(x_prompt.reshape(batch, seq // CHUNK, SUBLANES, NT, d), 2, 3).reshape(rows_p, d)
    ys = jnp.swapaxes(x_sample, 0, 1).reshape(rows_s, d)
    tn_in = _tile(n_main, 1024, LANES)
    fh = w_gate.shape[2]
    tf = _tile(fh, 512, LANES)
    state_all = state_ssm.reshape(depth, sq, I, N)

    ca_p, ssm_p, cc_p, ca_s, cc_s, v_s = [], [], [], [], [], []
    h_acc = None
    for i, lw in enumerate(layers):
        last = i == depth - 1
        proj, dt_raw = _inproj(yp, lw["norm_mix_w"], w_in_main, i, lw["wdh"], lw["wdl"],
                               tm=_tile(rows_p, 1024), tn=tn_in)
        ya, cst_a, h_new, yb, yc, cst_c = _mixers_prompt(proj, dt_raw, lw, cfg, batch, seq)
        mg = _merge(ya, yb, yc, proj, d, ws, i, cfg, tm=_tile(rows_p, 1024), tn=_tile(d, 256, LANES))
        yp = _ffn(yp, mg, lw["norm_ffn_w"], ws, i, final_w, tm=_tile(rows_p, 512), tf=tf, final=last)
        ca_p.append(cst_a[:, SUBLANES - 1::SUBLANES, :])
        ssm_p.append(h_new.reshape(batch, H, P, N))
        cc_p.append(cst_c[:, SUBLANES - 1::SUBLANES, :])

        proj, dt_raw = _inproj(ys, lw["norm_mix_w"], w_in_main, i, lw["wdh"], lw["wdl"],
                               tm=_tile(rows_s, 512), tn=tn_in)
        proj3 = proj.reshape(t_dec, sq, n_main)
        dt3 = dt_raw.reshape(t_dec, sq, LANES)
        ypart, xdl, eae, bs, cs, cd, nx, nb, nc = _ssd_sample_a(
            proj3, dt3, jnp.swapaxes(state_conv_a[i], 0, 1), lw, cfg)
        ya_t, h_acc = _ssd_sample_b(
            cd[:, :H].reshape(sq * H), state_all, i, h_acc,
            _pad_rows(cs, BF16_ROWS), _pad_rows(bs, BF16_ROWS), _pad_rows(xdl.astype(BF16), BF16_ROWS),
            _pad_rows(eae, SUBLANES), _pad_rows(ypart, SUBLANES), _pad_rows(proj3[:, :, :I], SUBLANES),
            lw["ssm_norm_w"], cfg, _tile(sq, 32))
        ya = jnp.swapaxes(ya_t[:, :t_dec, :], 0, 1).reshape(rows_s, I).astype(BF16)
        yb, yc, cst_c, v_norm = _mix_sample(proj3, jnp.swapaxes(state_conv_c[i], 0, 1), lw, cfg,
                                            _tile(W, 512, LANES))
        mg = _merge(ya, yb.reshape(rows_s, W), yc.reshape(rows_s, SCW), proj, d, ws, i, cfg,
                    tm=_tile(rows_s, 1024), tn=_tile(d, 256, LANES))
        ys = _ffn(ys, mg, lw["norm_ffn_w"], ws, i, final_w, tm=_tile(rows_s, 512), tf=tf, final=last)
        ca_s.append(jnp.swapaxes(jnp.concatenate([nx, nb, nc], axis=-1), 0, 1))
        cc_s.append(jnp.swapaxes(cst_c, 0, 1))
        v_s.append(jnp.swapaxes(v_norm, 0, 1))

    y_prompt = jnp.swapaxes(yp.reshape(batch, seq // CHUNK, NT, SUBLANES, d), 2, 3).reshape(batch, seq, d)
    y_sample = jnp.swapaxes(ys.reshape(t_dec, sq, d), 0, 1)
    return (y_prompt, y_sample, jnp.stack(ca_p), jnp.stack(ssm_p), jnp.stack(cc_p),
            jnp.stack(ca_s), h_acc.reshape(depth, sq, H, P, N), jnp.stack(cc_s), jnp.stack(v_s))
```

```python
import functools
import math

import jax
import jax.numpy as jnp
from jax import lax
from jax.experimental import pallas as pl
from jax.experimental.pallas import tpu as pltpu

F32 = jnp.float32
BF16 = jnp.bfloat16

RMS_EPS = 1e-6
LN_EPS = 1e-5
PAST_LEN = 16384
CHUNK = 128
LANES = 128
SUBLANE_BITS = 3
SUBLANES = 1 << SUBLANE_BITS
BF16_ROWS = 16
NEG_BIG = -1e30
VMEM_LIMIT_BYTES = 56 * 1024 * 1024


def _cparams(*sem):
    return pltpu.CompilerParams(dimension_semantics=sem, vmem_limit_bytes=VMEM_LIMIT_BYTES)


def _dot(a, b):
    return jnp.dot(a, b, preferred_element_type=F32)


def _dot_nt(a, b):
    return lax.dot_general(a, b, (((1,), (1,)), ((), ())), preferred_element_type=F32)


def _dot_tn(a, b):
    return lax.dot_general(a, b, (((0,), (0,)), ((), ())), preferred_element_type=F32)


def _split3(x):
    hi = x.astype(BF16)
    r = x - hi.astype(F32)
    mid = r.astype(BF16)
    lo = (r - mid.astype(F32)).astype(BF16)
    return hi, mid, lo


def _dot_exact_rhs(x, m_bf16):
    hi, mid, lo = _split3(x)
    return _dot(hi, m_bf16) + _dot(mid, m_bf16) + _dot(lo, m_bf16)


def _dot_exact_lhs(m_bf16, x):
    hi, mid, lo = _split3(x)
    return _dot(m_bf16, hi) + _dot(m_bf16, mid) + _dot(m_bf16, lo)


def _sigmoid(x):
    return 0.5 * jnp.tanh(0.5 * x) + 0.5


def _silu(x):
    return x * _sigmoid(x)


def _softplus(x):
    return jnp.maximum(x, 0.0) + jnp.log1p(jnp.exp(-jnp.abs(x)))


def _rms(x, w):
    return x * lax.rsqrt(jnp.mean(x * x, axis=-1, keepdims=True) + RMS_EPS) * w


def _causal_taps_interleaved(v_tiles, w, carry_ref, cs):
    k = w.shape[0]
    sub = lax.broadcasted_iota(jnp.int32, v_tiles[0].shape, 0)
    acc = [v * w[0:1, :] for v in v_tiles]
    for j in range(1, k):
        prev_last = carry_ref[j - 1, :, cs]
        carry_ref[j - 1, :, cs] = acc[-1]
        first = jnp.where(sub == 0, pltpu.roll(prev_last, 1, axis=0), pltpu.roll(acc[-1], 1, axis=0))
        acc = [first] + acc[:-1]
        acc = [a + v * w[j:j + 1, :] for a, v in zip(acc, v_tiles)]
    return jnp.concatenate(acc, axis=0)


def _layernorm(x, w, b):
    mu = jnp.mean(x, axis=-1, keepdims=True)
    d = x - mu
    var = jnp.mean(d * d, axis=-1, keepdims=True)
    return d * lax.rsqrt(var + LN_EPS) * w + b


def _inproj_body(x_ref, nw_ref, w_ref, wdh_ref, wdl_ref, o_ref, dt_ref, h_scr, *, sub):
    tm = x_ref.shape[0]

    @pl.when(pl.program_id(1) == 0)
    def _():
        def rows(i, carry):
            r = pl.ds(pl.multiple_of(i * sub, sub), sub)
            h = _rms(x_ref[r, :], nw_ref[...])
            hb = h.astype(BF16)
            h_scr[r, :] = hb
            hl = (h - hb.astype(F32)).astype(BF16)
            dt_ref[r, :] = _dot(hb, wdh_ref[...]) + _dot(hb, wdl_ref[...]) + _dot(hl, wdh_ref[...])
            return carry
        lax.fori_loop(0, tm // sub, rows, 0)

    o_ref[...] = _dot(h_scr[...], w_ref[...])


def _prep_w_in_body(a_ref, b_ref, o_ref, *, j0, shift, sub):
    tw, d = a_ref.shape
    j = pl.program_id(1)

    @pl.when(j < j0)
    def _():
        for k in range(d // sub):
            ks = slice(k * sub, (k + 1) * sub)
            o_ref[ks, :] = a_ref[:, ks].T.astype(BF16)

    @pl.when(j >= j0)
    def _():
        for k in range(d // sub):
            ks = slice(k * sub, (k + 1) * sub)
            src = jnp.concatenate([a_ref[shift:, ks], b_ref[:, ks]], axis=0)
            o_ref[ks, :] = src.T.astype(BF16)


def _prep_w_in(w_in_t, dt0, shift, n_main, tw):
    depth, _, d = w_in_t.shape
    assert dt0 % tw == 0 and n_main % tw == 0 and tw % shift == 0 and shift % SUBLANES == 0
    j0 = dt0 // tw
    per = tw // shift
    return pl.pallas_call(
        functools.partial(_prep_w_in_body, j0=j0, shift=shift, sub=min(d, 256)),
        grid=(depth, n_main // tw),
        in_specs=[
            pl.BlockSpec((None, tw, d), lambda l, j: (l, j, 0)),
            pl.BlockSpec((None, shift, d), lambda l, j: (l, jnp.where(j < j0, 0, per * (j + 1)), 0)),
        ],
        out_specs=pl.BlockSpec((None, d, tw), lambda l, j: (l, 0, j)),
        out_shape=jax.ShapeDtypeStruct((depth, d, n_main), BF16),
        compiler_params=_cparams("parallel", "parallel"),
        name="prep_w_in",
    )(w_in_t, w_in_t)


def _prep_w_dt_body(a_ref, hi_ref, lo_ref, *, sub):
    h, d = a_ref.shape
    for k in range(d // sub):
        ks = slice(k * sub, (k + 1) * sub)
        src = jnp.concatenate([a_ref[:, ks], jnp.zeros((LANES - h, sub), F32)], axis=0)
        w = src.T
        hi = w.astype(BF16)
        hi_ref[ks, :] = hi
        lo_ref[ks, :] = (w - hi.astype(F32)).astype(BF16)


def _prep_w_dt(w_in_t, dt0, h):
    depth, _, d = w_in_t.shape
    assert dt0 % h == 0 and h % SUBLANES == 0 and h <= LANES
    out = pl.BlockSpec((None, d, LANES), lambda l: (l, 0, 0))
    return pl.pallas_call(
        functools.partial(_prep_w_dt_body, sub=min(d, 256)),
        grid=(depth,),
        in_specs=[pl.BlockSpec((None, h, d), lambda l: (l, dt0 // h, 0))],
        out_specs=[out, out],
        out_shape=[jax.ShapeDtypeStruct((depth, d, LANES), BF16)] * 2,
        compiler_params=_cparams("parallel"),
        name="prep_w_dt",
    )(w_in_t)


def _inproj(x, nw, w, layer, wdh, wdl, *, tm, tn):
    rows, d = x.shape
    n = w.shape[2]
    return pl.pallas_call(
        functools.partial(_inproj_body, sub=min(tm, 128)),
        grid=(rows // tm, n // tn),
        in_specs=[
            pl.BlockSpec((tm, d), lambda m, j: (m, 0), pipeline_mode=pl.Buffered(1)),
            pl.BlockSpec((1, d), lambda m, j: (0, 0)),
            pl.BlockSpec((None, d, tn), lambda m, j: (layer, 0, j)),
            pl.BlockSpec((d, LANES), lambda m, j: (0, 0)),
            pl.BlockSpec((d, LANES), lambda m, j: (0, 0)),
        ],
        out_specs=[
            pl.BlockSpec((tm, tn), lambda m, j: (m, j)),
            pl.BlockSpec((tm, LANES), lambda m, j: (m, 0)),
        ],
        out_shape=[jax.ShapeDtypeStruct((rows, n), F32), jax.ShapeDtypeStruct((rows, LANES), F32)],
        scratch_shapes=[pltpu.VMEM((tm, d), BF16)],
        compiler_params=_cparams("parallel", "arbitrary"),
        name="inproj",
    )(x, nw, w, wdh, wdl)


def _mixers_prompt_body(z_ref, x_ref, bc_ref, dt_ref, cw_ref, cb_ref, dtb_ref, alog_ref, dsk_ref, nw_ref,
                        e_ref, u_ref, v_ref, sb_ref, sc_ref, sh_ref, lnw_ref, lnb_ref, ws_ref, be_ref, ccw_ref,
                        ya_ref, cst_ref, h_ref, yb_ref, yc_ref, cstc_ref, cx, cbc, cq, vb_scr,
                        *, G, HG, P, N):
    c = pl.program_id(1)
    L = CHUNK
    NT = L // SUBLANES
    GW = HG * P
    I = G * GW
    W = u_ref.shape[1]
    SG = ws_ref.shape[0]
    GWS = W // SG
    K = cw_ref.shape[0]
    KC = ccw_ref.shape[0]

    @pl.when(c == 0)
    def _():
        h_ref[...] = jnp.zeros(h_ref.shape, F32)
        cx[...] = jnp.zeros(cx.shape, F32)
        cbc[...] = jnp.zeros(cbc.shape, F32)
        cq[...] = jnp.zeros(cq.shape, F32)

    vb_scr[...] = _layernorm(jax.nn.gelu(v_ref[...]), lnw_ref[...], lnb_ref[...]).astype(BF16)

    def tile(ref, k, cols):
        return ref[k * SUBLANES:(k + 1) * SUBLANES, cols]

    def gmlp_and_short_conv(g):
        gs = slice(g * GWS, (g + 1) * GWS)
        mixed = _dot(ws_ref[g].astype(BF16), vb_scr[:, gs]) + be_ref[:, gs]
        yb_ref[:, gs] = (jax.nn.gelu(u_ref[:, gs]) * mixed).astype(BF16)
        q = [tile(sc_ref, k, gs) * tile(sh_ref, k, gs) for k in range(NT)]
        conv_c = _causal_taps_interleaved(q, ccw_ref[:, gs], cq, gs)
        yc_ref[:, gs] = (sb_ref[:, gs] * conv_c).astype(BF16)
        for i in range(KC - 1):
            cstc_ref[0, i * SUBLANES:(i + 1) * SUBLANES, gs] = q[NT - (KC - 1) + i]

    def conv(ref, carry, lo, hi, col0):
        cs = slice(lo, hi)
        acc = _causal_taps_interleaved([tile(ref, k, cs) for k in range(NT)],
                                       cw_ref[:, col0 + lo:col0 + hi], carry, cs)
        return _silu(acc + cb_ref[:, col0 + lo:col0 + hi])

    dt = _softplus(dt_ref[...] + dtb_ref[...])
    a = -jnp.exp(alog_ref[...])
    row = lax.broadcasted_iota(jnp.int32, (L, L), 0)
    col = lax.broadcasted_iota(jnp.int32, (L, L), 1)
    time_of = lambda r: (r & (SUBLANES - 1)) * NT + (r >> SUBLANE_BITS)
    tri = time_of(row) >= time_of(col)
    tril_ones = jnp.where(tri, 1.0, 0.0).astype(BF16)
    acum = _dot_exact_lhs(tril_ones, dt * a)
    acum_t = acum.T
    last = acum[L - 1:L, :]
    dl = jnp.exp(last - acum)
    ea = jnp.exp(acum)
    cd_t = jnp.exp(acum_t[:, L - 1:L])
    dt_s, dl_s, ea_s = _split3(dt), _split3(dl), _split3(ea)
    lane = lax.broadcasted_iota(jnp.int32, (L, LANES), 1)
    HPL = LANES // P

    for g in range(G):
        gs = slice(g * GW, (g + 1) * GW)
        e_g = e_ref[:, gs]

        def expand(parts):
            return _dot(parts[0], e_g) + _dot(parts[1], e_g) + _dot(parts[2], e_g)

        xg = conv(x_ref, cx, g * GW, (g + 1) * GW, 0)
        xdt = xg * expand(dt_s)
        bb = conv(bc_ref, cbc, g * N, (g + 1) * N, I).astype(BF16)
        cb16 = conv(bc_ref, cbc, (G + g) * N, (G + g + 1) * N, I).astype(BF16)
        cbm = _dot_nt(cb16, bb)
        hg16 = h_ref[0, gs, :].astype(BF16)
        y = _dot_nt(cb16, hg16) * expand(ea_s) + dsk_ref[:, gs] * xg

        tiles = []
        for j in range(GW // LANES):
            xt = xdt[:, j * LANES:(j + 1) * LANES]
            acc = jnp.zeros((L, LANES), F32)
            for hh in range(HPL):
                h = g * HG + j * HPL + hh
                seg = acum[:, h:h + 1] - acum_t[h:h + 1, :]
                dec = jnp.exp(jnp.where(tri, seg, NEG_BIG))
                m = (cbm * dec).astype(BF16)
                inhead = (lane >= hh * P) & (lane < (hh + 1) * P)
                acc = acc + _dot(m, jnp.where(inhead, xt, 0.0).astype(BF16))
            tiles.append(acc)
        y = y + jnp.concatenate(tiles, axis=1)

        gt = y * _silu(z_ref[:, gs])
        ya_ref[:, gs] = _rms(gt, nw_ref[:, gs]).astype(BF16)

        s_new = _dot_tn((xdt * expand(dl_s)).astype(BF16), bb)
        for hh in range(HG):
            h = g * HG + hh
            rs = slice(g * GW + hh * P, g * GW + (hh + 1) * P)
            h_ref[0, rs, :] = h_ref[0, rs, :] * cd_t[h:h + 1, :] + s_new[hh * P:(hh + 1) * P, :]

        if g < SG:
            gmlp_and_short_conv(g)
    for g in range(G, SG):
        gmlp_and_short_conv(g)

    tail = slice((NT - (K - 1)) * SUBLANES, L)
    cst_ref[0, :, 0:I] = x_ref[tail, :]
    cst_ref[0, :, I:] = bc_ref[tail, :]


def _mixers_prompt(proj, dt_raw, lw, cfg, batch, seq):
    G, HG, P, N, W = cfg["G"], cfg["HG"], cfg["P"], cfg["N"], cfg["W"]
    I = G * HG * P
    BC = 2 * G * N
    CD = I + BC
    nc = seq // CHUNK
    off = cfg["off"]
    ka = (lw["conv_a_w"].shape[0] - 1) * SUBLANES
    kc = (lw["conv_c_w"].shape[0] - 1) * SUBLANES
    row = lambda b, c: b * nc + c
    full = lambda b, c: (0, 0)
    tok = lambda name: pl.BlockSpec((CHUNK, W), lambda b, c: (row(b, c), off[name] // W))
    return pl.pallas_call(
        functools.partial(_mixers_prompt_body, G=G, HG=HG, P=P, N=N),
        grid=(batch, nc),
        in_specs=[
            pl.BlockSpec((CHUNK, I), lambda b, c: (row(b, c), off["z"] // I)),
            pl.BlockSpec((CHUNK, I), lambda b, c: (row(b, c), off["x"] // I)),
            pl.BlockSpec((CHUNK, BC), lambda b, c: (row(b, c), off["bc"] // BC)),
            pl.BlockSpec((CHUNK, LANES), lambda b, c: (row(b, c), 0)),
            pl.BlockSpec(lw["conv_a_w"].shape, full),
            pl.BlockSpec((1, CD), full),
            pl.BlockSpec((1, LANES), full),
            pl.BlockSpec((1, LANES), full),
            pl.BlockSpec((1, I), full),
            pl.BlockSpec((1, I), full),
            pl.BlockSpec((LANES, I), full),
            tok("u"), tok("v"), tok("scb"), tok("scc"), tok("sch"),
            pl.BlockSpec((1, W), full), pl.BlockSpec((1, W), full),
            pl.BlockSpec(lw["w_spatial"].shape, lambda b, c: (0, 0, 0)),
            pl.BlockSpec((CHUNK, W), full),
            pl.BlockSpec(lw["conv_c_w"].shape, full),
        ],
        out_specs=[
            pl.BlockSpec((CHUNK, I), lambda b, c: (row(b, c), 0)),
            pl.BlockSpec((1, ka, CD), lambda b, c: (b, 0, 0)),
            pl.BlockSpec((1, I, N), lambda b, c: (b, 0, 0)),
            pl.BlockSpec((CHUNK, W), lambda b, c: (row(b, c), 0)),
            pl.BlockSpec((CHUNK, W), lambda b, c: (row(b, c), 0)),
            pl.BlockSpec((1, kc, W), lambda b, c: (b, 0, 0)),
        ],
        out_shape=[
            jax.ShapeDtypeStruct((batch * seq, I), BF16),
            jax.ShapeDtypeStruct((batch, ka, CD), F32),
            jax.ShapeDtypeStruct((batch, I, N), F32),
            jax.ShapeDtypeStruct((batch * seq, W), BF16),
            jax.ShapeDtypeStruct((batch * seq, W), BF16),
            jax.ShapeDtypeStruct((batch, kc, W), F32),
        ],
        scratch_shapes=[
            pltpu.VMEM((ka // SUBLANES, SUBLANES, I), F32),
            pltpu.VMEM((ka // SUBLANES, SUBLANES, BC), F32),
            pltpu.VMEM((kc // SUBLANES, SUBLANES, W), F32),
            pltpu.VMEM((CHUNK, W), BF16),
        ],
        compiler_params=_cparams("parallel", "arbitrary"),
        name="mixers_prompt",
    )(proj, proj, proj, dt_raw, lw["conv_a_w"], lw["conv_a_b"], lw["dt_bias"], lw["a_log"],
      lw["d_skip_e"], lw["ssm_norm_w"], lw["expand"],
      proj, proj, proj, proj, proj, lw["sgu_ln_w"], lw["sgu_ln_b"], lw["w_spatial"], lw["b_spatial_e"],
      lw["conv_c_w"])


def _ssd_sample_a_body(x_ref, b_ref, c_ref, dt_ref, px_ref, pb_ref, pc_ref, cwx_ref, cwb_ref, cwc_ref,
                       cbx_ref, cbb_ref, cbc_ref, dtb_ref, alog_ref, dsk_ref, e_ref,
                       yp_ref, xdl_ref, ea_ref, bs_ref, cs_ref, cd_ref, nx_ref, nb_ref, nc_ref):
    T = x_ref.shape[0]
    K = cwx_ref.shape[0]

    def conv(p_ref, cur_ref, w_ref, bias_ref, new_ref):
        rows = [p_ref[j] for j in range(K - 1)] + [cur_ref[l] for l in range(T)]
        for j in range(K - 1):
            new_ref[j] = rows[T + j]
        outs = []
        for l in range(T):
            acc = rows[l] * w_ref[0:1, :]
            for j in range(1, K):
                acc = acc + rows[l + j] * w_ref[j:j + 1, :]
            outs.append(_silu(acc + bias_ref[...]))
        return outs

    xs = conv(px_ref, x_ref, cwx_ref, cbx_ref, nx_ref)
    bs = conv(pb_ref, b_ref, cwb_ref, cbb_ref, nb_ref)
    cs = conv(pc_ref, c_ref, cwc_ref, cbc_ref, nc_ref)

    a = -jnp.exp(alog_ref[...])
    dts, acums = [], []
    run = None
    for l in range(T):
        dt = _softplus(dt_ref[l] + dtb_ref[...])
        run = dt * a if run is None else run + dt * a
        dts.append(dt)
        acums.append(run)
    last = acums[T - 1]
    cd_ref[...] = jnp.exp(last)

    e_g = e_ref[...]

    def expand(coef):
        return _dot_exact_rhs(coef, e_g)

    xdts = [xs[l] * expand(dts[l]) for l in range(T)]
    for l in range(T):
        bs_ref[l] = bs[l].astype(BF16)
        cs_ref[l] = cs[l].astype(BF16)
        ea_ref[l] = expand(jnp.exp(acums[l]))
        xdl_ref[l] = xdts[l] * expand(jnp.exp(last - acums[l]))
        y = dsk_ref[...] * xs[l]
        for s in range(l + 1):
            cb = jnp.sum(cs[l] * bs[s], axis=-1, keepdims=True)
            y = y + expand(cb * jnp.exp(acums[l] - acums[s])) * xdts[s]
        yp_ref[l] = y


def _ssd_sample_a(proj3, dt3, conv_tm, lw, cfg):
    G, HG, P, N = cfg["G"], cfg["HG"], cfg["P"], cfg["N"]
    GW = HG * P
    I = G * GW
    T, SQ, _ = proj3.shape
    K = lw["conv_a_w"].shape[0]
    off = cfg["off"]
    xo, bo, co = off["x"] // GW, off["bc"] // N, off["bc"] // N + G
    sxo, sbo, sco = 0, I // N, I // N + G
    tok = lambda w, o: pl.BlockSpec((T, SQ, w), lambda g: (0, 0, o + g))
    pre = lambda w, o: pl.BlockSpec((K - 1, SQ, w), lambda g: (0, 0, o + g))
    par = lambda r, w, o: pl.BlockSpec((r, w), lambda g: (0, o + g))
    fixed = lambda shape: pl.BlockSpec(shape, lambda g: tuple(0 for _ in shape))
    return pl.pallas_call(
        _ssd_sample_a_body,
        grid=(G,),
        in_specs=[
            tok(GW, xo), tok(N, bo), tok(N, co), fixed((T, SQ, LANES)),
            pre(GW, sxo), pre(N, sbo), pre(N, sco),
            par(K, GW, sxo), par(K, N, sbo), par(K, N, sco),
            par(1, GW, sxo), par(1, N, sbo), par(1, N, sco),
            fixed((1, LANES)), fixed((1, LANES)), par(1, GW, 0), par(LANES, GW, 0),
        ],
        out_specs=[
            tok(GW, 0), tok(GW, 0), tok(GW, 0), tok(N, 0), tok(N, 0), fixed((SQ, LANES)),
            pre(GW, 0), pre(N, 0), pre(N, 0),
        ],
        out_shape=[
            jax.ShapeDtypeStruct((T, SQ, I), F32),
            jax.ShapeDtypeStruct((T, SQ, I), F32),
            jax.ShapeDtypeStruct((T, SQ, I), F32),
            jax.ShapeDtypeStruct((T, SQ, G * N), BF16),
            jax.ShapeDtypeStruct((T, SQ, G * N), BF16),
            jax.ShapeDtypeStruct((SQ, LANES), F32),
            jax.ShapeDtypeStruct((K - 1, SQ, I), F32),
            jax.ShapeDtypeStruct((K - 1, SQ, G * N), F32),
            jax.ShapeDtypeStruct((K - 1, SQ, G * N), F32),
        ],
        compiler_params=_cparams("arbitrary"),
        name="ssd_sample_a",
    )(proj3, proj3, proj3, dt3, conv_tm, conv_tm, conv_tm,
      lw["conv_a_w"], lw["conv_a_w"], lw["conv_a_w"], lw["conv_a_b"], lw["conv_a_b"], lw["conv_a_b"],
      lw["dt_bias"], lw["a_log"], lw["d_skip_e"], lw["expand"])


def _ssd_sample_b_body(cd_ref, h0_ref, cs_ref, bs_ref, xdl_ref, ea_ref, yp_ref, z_ref, nw_ref,
                       *rest, HG, P, H):
    ya_ref, hn_ref = rest[-2:]
    sb = pl.program_id(0)
    g = pl.program_id(1)
    SB = h0_ref.shape[0]
    TP = yp_ref.shape[1]
    for i in range(SB):
        hmat = h0_ref[i]
        yoff = _dot_nt(cs_ref[i], hmat.astype(BF16))
        y = yp_ref[i] + yoff[0:TP, :] * ea_ref[i]
        gt = y * _silu(z_ref[i])
        ya_ref[i] = _rms(gt, nw_ref[...])
        s_new = _dot_tn(xdl_ref[i], bs_ref[i])
        for hh in range(HG):
            cd = cd_ref[(sb * SB + i) * H + g * HG + hh]
            rs = slice(hh * P, (hh + 1) * P)
            hn_ref[i, rs, :] = hmat[rs, :] * cd + s_new[rs, :]


def _ssd_sample_b(cd_flat, h0_all, layer, h_acc, cs_t, bs_t, xdl_t, ea_t, yp_t, z_t, nw, cfg, sb):
    G, HG, P, N = cfg["G"], cfg["HG"], cfg["P"], cfg["N"]
    GW = HG * P
    _, SQ, I, _ = h0_all.shape
    TP = yp_t.shape[1]
    tok = lambda r, w: pl.BlockSpec((sb, r, w), lambda s, g: (s, 0, g))
    state = pl.BlockSpec((None, sb, GW, N), lambda s, g: (layer, s, g, 0))
    in_specs = [
        pl.BlockSpec(memory_space=pltpu.SMEM),
        state,
        tok(BF16_ROWS, N), tok(BF16_ROWS, N), tok(BF16_ROWS, GW),
        tok(TP, GW), tok(TP, GW), tok(TP, GW),
        pl.BlockSpec((1, GW), lambda s, g: (0, g)),
    ]
    args = [cd_flat, h0_all, cs_t, bs_t, xdl_t, ea_t, yp_t, z_t, nw]
    aliases = {}
    if h_acc is not None:
        in_specs.append(pl.BlockSpec(memory_space=pl.ANY))
        args.append(h_acc)
        aliases = {len(args) - 1: 1}
    return pl.pallas_call(
        functools.partial(_ssd_sample_b_body, HG=HG, P=P, H=G * HG),
        grid=(SQ // sb, G),
        in_specs=in_specs,
        out_specs=[tok(TP, GW), state],
        out_shape=[
            jax.ShapeDtypeStruct((SQ, TP, I), F32),
            jax.ShapeDtypeStruct(h0_all.shape, F32),
        ],
        input_output_aliases=aliases,
        compiler_params=_cparams("parallel", "parallel"),
        name="ssd_sample_b",
    )(*args)


def _mix_sample_stats_body(v_ref, lnw_ref, lnb_ref, vo_ref):
    for l in range(v_ref.shape[0]):
        vo_ref[l] = _layernorm(jax.nn.gelu(v_ref[l]), lnw_ref[...], lnb_ref[...])


def _mix_sample_body(u_ref, v_ref, sb_ref, sc_ref, sh_ref, pq_ref, we_ref, be_ref, cw_ref,
                     yb_ref, yc_ref, cst_ref):
    T = u_ref.shape[0]
    K = cw_ref.shape[0]
    vs = [v_ref[l] for l in range(T)]
    for l in range(T):
        mixed = be_ref[l:l + 1, :]
        for s in range(T):
            mixed = mixed + we_ref[l * T + s:l * T + s + 1, :] * vs[s]
        yb_ref[l] = (jax.nn.gelu(u_ref[l]) * mixed).astype(BF16)
    rows = [pq_ref[j] for j in range(K - 1)] + [sc_ref[l] * sh_ref[l] for l in range(T)]
    for j in range(K - 1):
        cst_ref[j] = rows[T + j]
    for l in range(T):
        conv = rows[l] * cw_ref[0:1, :]
        for j in range(1, K):
            conv = conv + rows[l + j] * cw_ref[j:j + 1, :]
        yc_ref[l] = (sb_ref[l] * conv).astype(BF16)


def _mix_sample(proj3, convc_tm, lw, cfg, wc):
    W = cfg["W"]
    off = cfg["off"]
    T, SQ, _ = proj3.shape
    K = lw["conv_c_w"].shape[0]
    v_norm = pl.pallas_call(
        _mix_sample_stats_body,
        grid=(1,),
        in_specs=[
            pl.BlockSpec((T, SQ, W), lambda i: (0, 0, off["v"] // W)),
            pl.BlockSpec((1, W), lambda i: (0, 0)),
            pl.BlockSpec((1, W), lambda i: (0, 0)),
        ],
        out_specs=pl.BlockSpec((T, SQ, W), lambda i: (0, 0, 0)),
        out_shape=jax.ShapeDtypeStruct((T, SQ, W), F32),
        compiler_params=_cparams("arbitrary"),
        name="mix_sample_norm",
    )(proj3, lw["sgu_ln_w"], lw["sgu_ln_b"])
    tok = lambda name: pl.BlockSpec((T, SQ, wc), lambda j: (0, 0, off[name] // wc + j))
    loc = lambda r: pl.BlockSpec((r, SQ, wc), lambda j: (0, 0, j))
    par = lambda r: pl.BlockSpec((r, wc), lambda j: (0, j))
    yb, yc, cst = pl.pallas_call(
        _mix_sample_body,
        grid=(W // wc,),
        in_specs=[tok("u"), loc(T), tok("scb"), tok("scc"), tok("sch"), loc(K - 1),
                  par(T * T), par(T), par(K)],
        out_specs=[loc(T), loc(T), loc(K - 1)],
        out_shape=[
            jax.ShapeDtypeStruct((T, SQ, W), BF16),
            jax.ShapeDtypeStruct((T, SQ, W), BF16),
            jax.ShapeDtypeStruct((K - 1, SQ, W), F32),
        ],
        compiler_params=_cparams("parallel"),
        name="mix_sample",
    )(proj3, v_norm, proj3, proj3, proj3, convc_tm, lw["w_sub_e"], lw["b_sub_e"], lw["conv_c_w"])
    return yb, yc, cst, v_norm


def _merge_body(ya_ref, yb_ref, yc_ref, ga_ref, gb_ref, gc_ref, wa_ref, wb_ref, wc_ref, o_ref):
    merged = (_sigmoid(ga_ref[...]) * _dot(ya_ref[...], wa_ref[...])
              + _sigmoid(gb_ref[...]) * _dot(yb_ref[...], wb_ref[...])
              + _sigmoid(gc_ref[...]) * _dot(yc_ref[...], wc_ref[...]))
    o_ref[...] = merged.astype(BF16)


def _merge(ya, yb, yc, proj, d, ws, layer, cfg, *, tm, tn):
    rows = ya.shape[0]
    I, W, SCW = ya.shape[1], yb.shape[1], yc.shape[1]
    go = cfg["off"]["gate"] // tn
    act = lambda w: pl.BlockSpec((tm, w), lambda m, j: (m, 0))
    gate = lambda k: pl.BlockSpec((tm, tn), lambda m, j: (m, go + k * (d // tn) + j))
    wcol = lambda k: pl.BlockSpec((None, k, tn), lambda m, j: (layer, 0, j))
    return pl.pallas_call(
        _merge_body,
        grid=(rows // tm, d // tn),
        in_specs=[act(I), act(W), act(SCW), gate(0), gate(1), gate(2), wcol(I), wcol(W), wcol(SCW)],
        out_specs=pl.BlockSpec((tm, tn), lambda m, j: (m, j)),
        out_shape=jax.ShapeDtypeStruct((rows, d), BF16),
        compiler_params=_cparams("parallel", "parallel"),
        name="merge",
    )(ya, yb, yc, proj, proj, proj, ws["w_out_a"], ws["w_out_b"], ws["w_out_c"])


def _ffn_body(x_ref, mg_ref, wo_ref, nw_ref, wg_ref, wu_ref, wd_ref, fw_ref, o_ref, h_scr, *, sub, final):
    f = pl.program_id(1)
    tm = x_ref.shape[0]

    @pl.when(f == 0)
    def _():
        o_ref[...] = x_ref[...] + _dot(mg_ref[...], wo_ref[...])

        def rows(i, carry):
            r = pl.ds(pl.multiple_of(i * sub, sub), sub)
            h_scr[r, :] = _rms(o_ref[r, :], nw_ref[...]).astype(BF16)
            return carry
        lax.fori_loop(0, tm // sub, rows, 0)

    h = h_scr[...]
    act = _silu(_dot(h, wg_ref[...])) * _dot(h, wu_ref[...])
    o_ref[...] += _dot(act.astype(BF16), wd_ref[...])

    if final:
        @pl.when(f == pl.num_programs(1) - 1)
        def _():
            def rows(i, carry):
                r = pl.ds(pl.multiple_of(i * sub, sub), sub)
                o_ref[r, :] = _rms(o_ref[r, :], fw_ref[...])
                return carry
            lax.fori_loop(0, tm // sub, rows, 0)


def _ffn(x, merged, nw, ws, layer, final_w, *, tm, tf, final):
    rows, d = x.shape
    fh = ws["w_gate"].shape[2]
    act = pl.BlockSpec((tm, d), lambda m, f: (m, 0))
    vec = pl.BlockSpec((1, d), lambda m, f: (0, 0))
    wcol = pl.BlockSpec((None, d, tf), lambda m, f: (layer, 0, f))
    wo = pl.BlockSpec((None, d, d), lambda m, f: (layer, 0, 0), pipeline_mode=pl.Buffered(1))
    return pl.pallas_call(
        functools.partial(_ffn_body, sub=min(tm, 128), final=final),
        grid=(rows // tm, fh // tf),
        in_specs=[act, act, wo, vec, wcol, wcol, pl.BlockSpec((None, tf, d), lambda m, f: (layer, f, 0)), vec],
        out_specs=act,
        out_shape=jax.ShapeDtypeStruct((rows, d), F32),
        scratch_shapes=[pltpu.VMEM((tm, d), BF16)],
        compiler_params=_cparams("parallel", "arbitrary"),
        name="ffn",
    )(x, merged, ws["w_o"], nw, ws["w_gate"], ws["w_up"], ws["w_down"], final_w)


def _tile(n, pref, quantum=SUBLANES):
    for t in range(min(n, pref), 0, -1):
        if n % t == 0 and t % quantum == 0:
            return t
    raise ValueError((n, pref, quantum))


def _pad_rows(a, rows):
    a = jnp.swapaxes(a, 0, 1)
    return jnp.pad(a, ((0, 0), (0, rows - a.shape[1]), (0, 0)))


def kernel(x_prompt, x_sample, state_conv_a, state_ssm, state_conv_c, norm_mix_w, w_in, conv_a_w, conv_a_b,
           dt_bias, a_log, d_skip, ssm_norm_w, w_out_a, sgu_ln_w, sgu_ln_b, w_spatial, b_spatial, w_out_b,
           conv_c_w, w_out_c, w_o, norm_ffn_w, w_gate, w_up, w_down, norm_final_w):
    depth, d, _ = w_in.shape
    batch, seq, _ = x_prompt.shape
    sq, t_dec, _ = x_sample.shape
    I = ssm_norm_w.shape[1]
    H = dt_bias.shape[1]
    P, N = state_ssm.shape[3], state_ssm.shape[4]
    CD = conv_a_w.shape[2]
    G = (CD - I) // (2 * N)
    HG = H // G
    W = sgu_ln_w.shape[1]
    SG = w_spatial.shape[1]
    SCW = conv_c_w.shape[2]
    assert W == SCW and H <= LANES and LANES % P == 0 and (HG * P) % LANES == 0
    assert seq % CHUNK == 0 and t_dec <= SUBLANES and w_spatial.shape[2] == CHUNK

    sizes = (("z", I), ("x", I), ("bc", CD - I), ("u", W), ("v", W), ("scb", SCW), ("scc", SCW), ("sch", SCW),
             ("gate", 3 * d))
    off, o = {}, 0
    for name, s in sizes:
        off[name] = o
        o += s
    n_main = o
    cfg = dict(G=G, HG=HG, P=P, N=N, W=W, off=off)
    dt0 = I + CD

    w_in_t = jnp.swapaxes(w_in, 1, 2)
    w_in_main = _prep_w_in(w_in_t, dt0, H, n_main, _tile(math.gcd(dt0, n_main), 512, LANES))
    wdt_hi, wdt_lo = _prep_w_dt(w_in_t, dt0, H)
    ws = dict(w_out_a=w_out_a.astype(BF16), w_out_b=w_out_b.astype(BF16), w_out_c=w_out_c.astype(BF16),
              w_o=w_o.astype(BF16), w_gate=w_gate.astype(BF16), w_up=w_up.astype(BF16),
              w_down=w_down.astype(BF16))

    head_of_chan = jnp.arange(I) // P
    expand = (jnp.arange(LANES)[:, None] == head_of_chan[None, :]).astype(BF16)
    pos = (PAST_LEN + jnp.arange(t_dec)) % CHUNK
    NT = CHUNK // SUBLANES
    time_of_row = (jnp.arange(CHUNK) % SUBLANES) * NT + jnp.arange(CHUNK) // SUBLANES
    padh = lambda v: jnp.pad(v, (0, LANES - H)).reshape(1, LANES)

    layers = []
    for i in range(depth):
        w_sub = jnp.tril(w_spatial[i])[:, pos[:, None], pos[None, :]]
        layers.append(dict(
            norm_mix_w=norm_mix_w[i].reshape(1, d), wdh=wdt_hi[i], wdl=wdt_lo[i],
            conv_a_w=conv_a_w[i], conv_a_b=conv_a_b[i].reshape(1, CD),
            dt_bias=padh(dt_bias[i]), a_log=padh(a_log[i]),
            d_skip_e=jnp.repeat(d_skip[i], P).reshape(1, I), ssm_norm_w=ssm_norm_w[i].reshape(1, I),
            expand=expand,
            sgu_ln_w=sgu_ln_w[i].reshape(1, W), sgu_ln_b=sgu_ln_b[i].reshape(1, W),
            w_spatial=jnp.tril(w_spatial[i])[:, time_of_row[:, None], time_of_row[None, :]],
            b_spatial_e=jnp.repeat(b_spatial[i].T[time_of_row], W // SG, axis=1),
            w_sub_e=jnp.repeat(jnp.transpose(w_sub, (1, 2, 0)).reshape(t_dec * t_dec, SG), W // SG, axis=1),
            b_sub_e=jnp.repeat(b_spatial[i][:, pos].T, W // SG, axis=1),
            conv_c_w=conv_c_w[i],
            norm_ffn_w=norm_ffn_w[i].reshape(1, d),
        ))
    final_w = norm_final_w.reshape(1, d)

    rows_p = batch * seq
    rows_s = sq * t_dec
    yp = jnp.swapaxes(x_prompt.reshape(batch, seq // CHUNK, SUBLANES, NT, d), 2, 3).reshape(rows_p, d)
    ys = jnp.swapaxes(x_sample, 0, 1).reshape(rows_s, d)
    tn_in = _tile(n_main, 2048, LANES)
    fh = w_gate.shape[2]
    tf = _tile(fh, 512, LANES)
    state_all = state_ssm.reshape(depth, sq, I, N)

    ca_p, ssm_p, cc_p, ca_s, cc_s, v_s = [], [], [], [], [], []
    h_acc = None
    for i, lw in enumerate(layers):
        last = i == depth - 1
        proj, dt_raw = _inproj(yp, lw["norm_mix_w"], w_in_main, i, lw["wdh"], lw["wdl"],
                               tm=_tile(rows_p, 1024), tn=tn_in)
        ya, cst_a, h_new, yb, yc, cst_c = _mixers_prompt(proj, dt_raw, lw, cfg, batch, seq)
        mg = _merge(ya, yb, yc, proj, d, ws, i, cfg, tm=_tile(rows_p, 1024), tn=_tile(d, 256, LANES))
        yp = _ffn(yp, mg, lw["norm_ffn_w"], ws, i, final_w, tm=_tile(rows_p, 512), tf=tf, final=last)
        ca_p.append(cst_a[:, SUBLANES - 1::SUBLANES, :])
        ssm_p.append(h_new.reshape(batch, H, P, N))
        cc_p.append(cst_c[:, SUBLANES - 1::SUBLANES, :])

        proj, dt_raw = _inproj(ys, lw["norm_mix_w"], w_in_main, i, lw["wdh"], lw["wdl"],
                               tm=_tile(rows_s, 512), tn=tn_in)
        proj3 = proj.reshape(t_dec, sq, n_main)
        dt3 = dt_raw.reshape(t_dec, sq, LANES)
        ypart, xdl, eae, bs, cs, cd, nx, nb, nc = _ssd_sample_a(
            proj3, dt3, jnp.swapaxes(state_conv_a[i], 0, 1), lw, cfg)
        ya_t, h_acc = _ssd_sample_b(
            cd[:, :H].reshape(sq * H), state_all, i, h_acc,
            _pad_rows(cs, BF16_ROWS), _pad_rows(bs, BF16_ROWS), _pad_rows(xdl.astype(BF16), BF16_ROWS),
            _pad_rows(eae, SUBLANES), _pad_rows(ypart, SUBLANES), _pad_rows(proj3[:, :, :I], SUBLANES),
            lw["ssm_norm_w"], cfg, _tile(sq, 32))
        ya = jnp.swapaxes(ya_t[:, :t_dec, :], 0, 1).reshape(rows_s, I).astype(BF16)
        yb, yc, cst_c, v_norm = _mix_sample(proj3, jnp.swapaxes(state_conv_c[i], 0, 1), lw, cfg,
                                            _tile(W, 512, LANES))
        mg = _merge(ya, yb.reshape(rows_s, W), yc.reshape(rows_s, SCW), proj, d, ws, i, cfg,
                    tm=_tile(rows_s, 1024), tn=_tile(d, 256, LANES))
        ys = _ffn(ys, mg, lw["norm_ffn_w"], ws, i, final_w, tm=_tile(rows_s, 512), tf=tf, final=last)
        ca_s.append(jnp.swapaxes(jnp.concatenate([nx, nb, nc], axis=-1), 0, 1))
        cc_s.append(jnp.swapaxes(cst_c, 0, 1))
        v_s.append(jnp.swapaxes(v_norm, 0, 1))

    y_prompt = jnp.swapaxes(yp.reshape(batch, seq // CHUNK, NT, SUBLANES, d), 2, 3).reshape(batch, seq, d)
    y_sample = jnp.swapaxes(ys.reshape(t_dec, sq, d), 0, 1)
    return (y_prompt, y_sample, jnp.stack(ca_p), jnp.stack(ssm_p), jnp.stack(cc_p),
            jnp.stack(ca_s), h_acc.reshape(depth, sq, H, P, N), jnp.stack(cc_s), jnp.stack(v_s))
```
